```python
import jax, jax.numpy as jnp
from jax import lax
import numpy as np

D_MODEL = 2048
BATCH = 4
SEQ = 8192
DEPTH = 2
DEC_BATCH = 4
DEC_SEQ = 2048
PAST_LEN = 128

W_A = 512
A_GROUPS = 4
A_GDIM = W_A // A_GROUPS
W_B = 512
B_GROUPS = 4
B_GDIM = W_B // B_GROUPS
B_CHUNK = 128
C_CONFIGS = ((128, 1), (512, 4), (2048, 16))
C_NG = len(C_CONFIGS)
C_HEADS = 8
C_HEAD_DIM = 64
W_C = C_HEADS * C_HEAD_DIM
C_QKV = C_NG * W_C
ROPE_DIM = C_HEAD_DIM // 4
ROPE_THETA = 500000.0
N_BRANCH = 3
W_BR = 512
ALPHA = (2 * DEPTH) ** 0.25
BETA = (8 * DEPTH) ** -0.25
LN_EPS = 1e-5
NEG_BIG = -1e30

OFF_A = 0
OFF_AZ = OFF_A + W_A
OFF_B = OFF_AZ + W_A
OFF_BZ = OFF_B + 2 * W_B
OFF_C = OFF_BZ + W_B
OFF_CZ = OFF_C + 3 * C_QKV
OFF_G = OFF_CZ + W_C
N_IN = OFF_G + N_BRANCH * D_MODEL

kernel_name = 'hybrid_fnet_gmlp_dilated_encoder'


def _layer_norm(x, g, b):
    xf = x.astype(jnp.float32)
    mu = jnp.mean(xf, axis=-1, keepdims=True)
    var = jnp.mean(jnp.square(xf - mu), axis=-1, keepdims=True)
    y = (xf - mu) * lax.rsqrt(var + LN_EPS) * g.astype(jnp.float32) + b.astype(jnp.float32)
    return y.astype(x.dtype)


def _rotary(t, pos):
    half = ROPE_DIM // 2
    inv_freq = ROPE_THETA ** (-jnp.arange(half, dtype=jnp.float32) / half)
    ang = pos[:, None] * inv_freq[None, :]
    cos = jnp.cos(ang)[None, :, None, :]
    sin = jnp.sin(ang)[None, :, None, :]
    tr = t[..., :ROPE_DIM].astype(jnp.float32)
    t1, t2 = tr[..., :half], tr[..., half:]
    rot = jnp.concatenate([t1 * cos - t2 * sin, t2 * cos + t1 * sin], axis=-1).astype(t.dtype)
    return jnp.concatenate([rot, t[..., ROPE_DIM:]], axis=-1)


def _fourier_mixer(a, w_map):
    B, S, _ = a.shape
    ag = a.reshape(B, S, A_GROUPS, A_GDIM).astype(jnp.float32)
    f = jnp.fft.fft2(ag, axes=(1, 3), norm='ortho').real.astype(a.dtype)
    out = jnp.einsum('bsgc,gcd->bsgd', f, w_map)
    return out.reshape(B, S, W_A)


def _spatial_gating(uv, ln_g, ln_b, w_s, b_s):
    B, S, _ = uv.shape
    uv = jax.nn.gelu(uv)
    u, v = uv[..., :W_B], uv[..., W_B:]
    v = _layer_norm(v, ln_g, ln_b)
    vc = v.reshape(B, S // B_CHUNK, B_CHUNK, B_GROUPS, B_GDIM)
    mixed = jnp.einsum('gpq,bnqgc->bnpgc', w_s, vc) + b_s.T[None, None, :, :, None]
    return u * mixed.reshape(B, S, W_B)


def _dilated_window_attention(q, k, v, dil, half):
    B, S, H, D = q.shape
    L = S // dil
    N = B * dil

    def split(t):
        return t.reshape(B, L, dil, H, D).transpose(0, 2, 1, 3, 4).reshape(N, L, H, D)

    qs, ks, vs = split(q), split(k), split(v)
    blk = half
    nb = -(-L // blk)
    Lp = nb * blk
    qb = jnp.pad(qs, ((0, 0), (0, Lp - L), (0, 0), (0, 0))).reshape(N, nb, blk, H, D)
    pad_kv = ((0, 0), (blk, Lp - L + blk), (0, 0), (0, 0))
    kb = jnp.pad(ks, pad_kv).reshape(N, nb + 2, blk, H, D)
    vb = jnp.pad(vs, pad_kv).reshape(N, nb + 2, blk, H, D)
    kw = jnp.concatenate([kb[:, :-2], kb[:, 1:-1], kb[:, 2:]], axis=2)
    vw = jnp.concatenate([vb[:, :-2], vb[:, 1:-1], vb[:, 2:]], axis=2)
    scale = 1.0 / np.sqrt(D).astype(np.float32)
    scores = jnp.einsum('nbqhd,nbkhd->nbhqk', qb, kw,
                        preferred_element_type=jnp.float32) * scale
    qpos = jnp.arange(nb)[:, None] * blk + jnp.arange(blk)[None, :]
    kpos = jnp.arange(nb)[:, None] * blk - blk + jnp.arange(3 * blk)[None, :]
    valid = ((jnp.abs(qpos[:, :, None] - kpos[:, None, :]) <= half)
             & (kpos[:, None, :] >= 0) & (kpos[:, None, :] < L))
    scores = jnp.where(valid[None, :, None], scores, NEG_BIG)
    lse = jax.nn.logsumexp(scores, axis=-1)
    p = jnp.exp(scores - lse[..., None])
    out = jnp.einsum('nbhqk,nbkhd->nbqhd', p, vw.astype(jnp.float32))
    out = out.reshape(N, Lp, H, D)[:, :L]
    lse = lse.transpose(0, 1, 3, 2).reshape(N, Lp, H)[:, :L]
    out = out.reshape(B, dil, L, H, D).transpose(0, 2, 1, 3, 4).reshape(B, S, H, D)
    lse = lse.reshape(B, dil, L, H).transpose(0, 2, 1, 3).reshape(B, S, H)
    return out, lse


def _dilated_mixer(qkv):
    B, S, _ = qkv.shape
    pos = jnp.arange(S, dtype=jnp.float32)
    q = _rotary(qkv[..., :C_QKV].reshape(B, S, C_NG * C_HEADS, C_HEAD_DIM), pos)
    k = _rotary(qkv[..., C_QKV:2 * C_QKV].reshape(B, S, C_NG * C_HEADS, C_HEAD_DIM), pos)
    v = qkv[..., 2 * C_QKV:].reshape(B, S, C_NG * C_HEADS, C_HEAD_DIM)
    outs, lses = [], []
    for g, (window, dil) in enumerate(C_CONFIGS):
        hs = slice(g * C_HEADS, (g + 1) * C_HEADS)
        o, l = _dilated_window_attention(q[:, :, hs], k[:, :, hs], v[:, :, hs], dil, window // (2 * dil))
        outs.append(o)
        lses.append(l)
    w = jax.nn.softmax(jnp.stack(lses, axis=0), axis=0)
    out = jnp.sum(w[..., None] * jnp.stack(outs, axis=0), axis=0)
    return out.reshape(B, S, W_C).astype(qkv.dtype)


def _encoder_layer(x, w_in, w_amap, sgu_ln_g, sgu_ln_b, sgu_w, sgu_b, w_branch, w_out, ln_g, ln_b):
    B, S, _ = x.shape
    h = x @ w_in
    y_a = _fourier_mixer(h[..., OFF_A:OFF_AZ], w_amap) * jax.nn.silu(h[..., OFF_AZ:OFF_B])
    y_b = _spatial_gating(h[..., OFF_B:OFF_BZ], sgu_ln_g, sgu_ln_b, sgu_w, sgu_b) * jax.nn.silu(h[..., OFF_BZ:OFF_C])
    y_c = _dilated_mixer(h[..., OFF_C:OFF_CZ]) * jax.nn.silu(h[..., OFF_CZ:OFF_G])
    ys = jnp.stack([y_a, y_b, y_c], axis=2)
    proj = jnp.einsum('bskc,kcd->bskd', ys, w_branch)
    gates = jax.nn.sigmoid(h[..., OFF_G:].reshape(B, S, N_BRANCH, D_MODEL))
    merged = jnp.sum(gates * proj, axis=2)
    out = merged @ w_out
    return _layer_norm(ALPHA * x + out, ln_g, ln_b)


def _trunk(x, w_in, w_amap, sgu_ln_g, sgu_ln_b, sgu_w, sgu_b, w_branch, w_out, ln_g, ln_b):
    for l in range(DEPTH):
        x = _encoder_layer(x, w_in[l], w_amap[l], sgu_ln_g[l], sgu_ln_b[l], sgu_w[l], sgu_b[l],
                           w_branch[l], w_out[l], ln_g[l], ln_b[l])
    return x


def setup_inputs(seed: int = 0) -> dict:
    key = jax.random.key(seed)
    ks = jax.random.split(key, 12)
    f32 = jnp.float32
    x_prompt = jax.random.normal(ks[0], (BATCH, SEQ, D_MODEL), f32)
    x_sample = jax.random.normal(ks[1], (DEC_BATCH, DEC_SEQ, D_MODEL), f32)
    w_in = jax.random.normal(ks[2], (DEPTH, D_MODEL, N_IN), f32) * D_MODEL ** -0.5
    w_amap = jax.random.normal(ks[3], (DEPTH, A_GROUPS, A_GDIM, A_GDIM), f32) * A_GDIM ** -0.5
    sgu_ln_g = 1.0 + 0.02 * jax.random.normal(ks[4], (DEPTH, W_B), f32)
    sgu_ln_b = 0.02 * jax.random.normal(ks[5], (DEPTH, W_B), f32)
    sgu_w = jax.random.normal(ks[6], (DEPTH, B_GROUPS, B_CHUNK, B_CHUNK), f32) * B_CHUNK ** -0.5
    sgu_b = 0.02 * jax.random.normal(ks[7], (DEPTH, B_GROUPS, B_CHUNK), f32)
    w_branch = jax.random.normal(ks[8], (DEPTH, N_BRANCH, W_BR, D_MODEL), f32) * (W_BR ** -0.5 * BETA)
    w_out = jax.random.normal(ks[9], (DEPTH, D_MODEL, D_MODEL), f32) * (D_MODEL ** -0.5 * BETA)
    ln_g = 1.0 + 0.02 * jax.random.normal(ks[10], (DEPTH, D_MODEL), f32)
    ln_b = 0.02 * jax.random.normal(ks[11], (DEPTH, D_MODEL), f32)
    return {'x_prompt': x_prompt, 'x_sample': x_sample, 'w_in': w_in, 'w_amap': w_amap,
            'sgu_ln_g': sgu_ln_g, 'sgu_ln_b': sgu_ln_b, 'sgu_w': sgu_w, 'sgu_b': sgu_b,
            'w_branch': w_branch, 'w_out': w_out, 'ln_g': ln_g, 'ln_b': ln_b}


def reference(x_prompt, x_sample, w_in, w_amap, sgu_ln_g, sgu_ln_b, sgu_w, sgu_b, w_branch, w_out, ln_g, ln_b):
    y_prompt = _trunk(x_prompt, w_in, w_amap, sgu_ln_g, sgu_ln_b, sgu_w, sgu_b, w_branch, w_out, ln_g, ln_b)
    y_sample = _trunk(x_sample, w_in, w_amap, sgu_ln_g, sgu_ln_b, sgu_w, sgu_b, w_branch, w_out, ln_g, ln_b)
    return (y_prompt, y_sample)
```

```python
import functools
import math

import jax
import jax.numpy as jnp
from jax import lax
from jax.experimental import pallas as pl
from jax.experimental.pallas import tpu as pltpu

F32 = jnp.float32
BF16 = jnp.bfloat16

D_MODEL = 2048
DEPTH = 2
W_BR = 512
A_GROUPS = 4
GDIM = 128
B_CHUNK = 128
C_DILATIONS = (1, 4, 16)
C_NG = len(C_DILATIONS)
C_HEADS = 8
C_HEAD_DIM = 64
C_QKV = C_NG * C_HEADS * C_HEAD_DIM
ROPE_DIM = C_HEAD_DIM // 4
ROPE_THETA = 500000.0
N_BRANCH = 3
ALPHA = (2 * DEPTH) ** 0.25
LN_EPS = 1e-5
NEG_BIG = -1e30
HALF_WIN = 64

LANES = 128
BF16_ROWS = 16
CB = 512
NSLAB = CB // LANES
OFF_A, OFF_AZ, OFF_B, OFF_BZ, OFF_C = 0, 512, 1024, 2048, 2560
OFF_CZ = OFF_C + 3 * C_QKV
OFF_G = OFF_CZ + W_BR
N_MIX_BLOCKS = 15
MISC_AZ, MISC_U, MISC_V, MISC_BZ, MISC_CZ = 0, 1, 2, 3, 4
N_MISC = 5
J_MISC = 1
J_QKV = J_MISC + N_MISC

VMEM_LIMIT = 56 * 1024 * 1024


def _params(*sem):
    return pltpu.CompilerParams(dimension_semantics=sem, vmem_limit_bytes=VMEM_LIMIT)


def _slabs(x):
    return [x[:, c * LANES:(c + 1) * LANES] for c in range(x.shape[1] // LANES)]


def _silu(h):
    return h * jax.nn.sigmoid(h)


def _gelu_tanh(h):
    c = math.sqrt(2.0 / math.pi)
    return h * (0.5 * (1.0 + jnp.tanh(c * (h + 0.044715 * (h * h * h)))))


def _rope_slab(hc, cos, sin_lo, sin_hi):
    up = pltpu.roll(hc, LANES - ROPE_DIM // 2, axis=1)
    dn = pltpu.roll(hc, ROPE_DIM // 2, axis=1)
    return hc * cos + up * sin_lo + dn * sin_hi


def _inproj_kernel(x_ref, w_ref, cos_ref, slo_ref, shi_ref,
                   ha_ref, misc_ref, g0_ref, g1_ref, g2_ref, xb_ref, acc_ref):
    j = pl.program_id(1)
    tm = x_ref.shape[0]

    @pl.when(j == 0)
    def _():
        xb_ref[...] = x_ref[...].astype(BF16)

    res = jnp.dot(xb_ref[...], w_ref[...], preferred_element_type=F32)
    for c, slab in enumerate(_slabs(res)):
        acc_ref[c] = slab

    def full():
        return jnp.concatenate([acc_ref[c] for c in range(NSLAB)], axis=1)

    @pl.when(j == 0)
    def _():
        ha_ref[...] = full()

    @pl.when((j == J_MISC + MISC_AZ) | (j == J_MISC + MISC_BZ) | (j == J_MISC + MISC_CZ))
    def _():
        misc_ref[...] = _silu(full()).astype(BF16)

    @pl.when((j == J_MISC + MISC_U) | (j == J_MISC + MISC_V))
    def _():
        misc_ref[...] = _gelu_tanh(full()).astype(BF16)

    for g, (dil, out_ref) in enumerate(zip(C_DILATIONS, (g0_ref, g1_ref, g2_ref))):
        n = tm // dil
        for part in range(3):
            @pl.when(j == J_QKV + 3 * g + part)
            def _(dil=dil, out_ref=out_ref, n=n, rotate=part < 2):
                for r in range(dil):
                    rows = pl.ds(r, n, stride=dil) if dil > 1 else slice(None)
                    for c in range(NSLAB):
                        hc = acc_ref[c, rows, :]
                        if rotate:
                            hc = _rope_slab(hc, cos_ref[rows, :], slo_ref[rows, :], shi_ref[rows, :])
                        if dil > 1:
                            out_ref[r, :, c * LANES:(c + 1) * LANES] = hc.astype(BF16)
                        else:
                            out_ref[:, c * LANES:(c + 1) * LANES] = hc.astype(BF16)


def _inproj(x2, w_mix, cos_t, slo_t, shi_t, batch, seq, tm):
    m = x2.shape[0]
    nseq = seq // tm
    tab = pl.BlockSpec((tm, LANES), lambda i, j: (i % nseq, 0))

    def part(j, first):
        return jnp.clip(j - first, 0, 2)

    def dil_spec(g):
        dil = C_DILATIONS[g]
        return pl.BlockSpec((None, dil, tm // dil, CB),
                            lambda i, j: (i // nseq, 0, i % nseq, part(j, J_QKV + 3 * g)))

    def dil_shape(g):
        dil = C_DILATIONS[g]
        return jax.ShapeDtypeStruct((batch, dil, seq // dil, 3 * CB), BF16)

    return pl.pallas_call(
        _inproj_kernel,
        grid=(m // tm, N_MIX_BLOCKS),
        in_specs=[
            pl.BlockSpec((tm, D_MODEL), lambda i, j: (i, 0)),
            pl.BlockSpec((D_MODEL, CB), lambda i, j: (0, j)),
            tab, tab, tab,
        ],
        out_specs=[
            pl.BlockSpec((tm, CB), lambda i, j: (i, 0)),
            pl.BlockSpec((tm, CB), lambda i, j: (i, jnp.clip(j - J_MISC, 0, N_MISC - 1))),
            pl.BlockSpec((tm, CB), lambda i, j: (i, part(j, J_QKV))),
            dil_spec(1), dil_spec(2),
        ],
        out_shape=[
            jax.ShapeDtypeStruct((m, CB), F32),
            jax.ShapeDtypeStruct((m, N_MISC * CB), BF16),
            jax.ShapeDtypeStruct((m, 3 * CB), BF16),
            dil_shape(1), dil_shape(2),
        ],
        scratch_shapes=[pltpu.VMEM((tm, D_MODEL), BF16), pltpu.VMEM((NSLAB, tm, LANES), F32)],
        compiler_params=_params("parallel", "arbitrary"),
        name="inproj",
    )(x2, w_mix, cos_t, slo_t, shi_t)


def _sgu_kernel(u_ref, v_ref, z_ref, g_ref, b_ref, ws_ref, bs_ref, o_ref):
    tm = u_ref.shape[0]
    v = v_ref[...].astype(F32)
    mu = jnp.mean(v, axis=-1, keepdims=True)
    vc = v - mu
    var = jnp.mean(vc * vc, axis=-1, keepdims=True)
    vn = (vc * lax.rsqrt(var + LN_EPS) * g_ref[...] + b_ref[...]).astype(BF16)
    bias = bs_ref[...]
    for n in range(tm // B_CHUNK):
        rows = slice(n * B_CHUNK, (n + 1) * B_CHUNK)
        mixed = jnp.concatenate(
            [jnp.dot(ws_ref[g], vn[rows, g * GDIM:(g + 1) * GDIM], preferred_element_type=F32)
             for g in range(A_GROUPS)], axis=1) + bias
        o_ref[rows, :] = (u_ref[rows, :].astype(F32) * mixed * z_ref[rows, :].astype(F32)).astype(BF16)


def _sgu(misc, ln_g, ln_b, ws_bf, bias_full, tm):
    m = misc.shape[0]
    col = lambda blk: pl.BlockSpec((tm, CB), lambda i: (i, blk))
    full = lambda shape: pl.BlockSpec(shape, lambda i: (0,) * len(shape))
    return pl.pallas_call(
        _sgu_kernel,
        grid=(m // tm,),
        in_specs=[col(MISC_U), col(MISC_V), col(MISC_BZ), full((1, CB)), full((1, CB)),
                  full((A_GROUPS, B_CHUNK, B_CHUNK)), full((B_CHUNK, CB))],
        out_specs=pl.BlockSpec((tm, CB), lambda i: (i, 0)),
        out_shape=jax.ShapeDtypeStruct((m, CB), BF16),
        compiler_params=_params("parallel"),
        name="sgu",
    )(misc, misc, misc, ln_g, ln_b, ws_bf, bias_full)


SA_ROWS = 8
SB_ROWS = BF16_ROWS


def _fft_a_kernel(x0_ref, x1_ref, x2_ref, x3_ref, f_ref, y_ref, xin_ref, yout_ref):
    n1 = x0_ref.shape[0]
    for g, x_ref in enumerate((x0_ref, x1_ref, x2_ref, x3_ref)):
        xin_ref[g] = x_ref[...].reshape(n1 * SA_ROWS, LANES)
    for jj in range(SA_ROWS):
        xj = jnp.concatenate([xin_ref[g, pl.ds(jj, n1, stride=SA_ROWS), :] for g in range(A_GROUPS)],
                             axis=1).astype(BF16)
        yj = jnp.dot(f_ref[...], xj, preferred_element_type=F32)
        for g, slab in enumerate(_slabs(yj)):
            yout_ref[g, pl.ds(jj, 2 * n1, stride=SA_ROWS), :] = slab
    for g in range(A_GROUPS):
        y_ref[g] = yout_ref[g].reshape(2 * n1, SA_ROWS, LANES)


def _fft_b_kernel(yr_ref, yi_ref, m_ref, z_ref, cs_ref, wm_ref, o_ref, f_ref, *, scale):
    n2 = yr_ref.shape[2]

    def body(kk, carry):
        ycat = jnp.concatenate(
            [jnp.concatenate([y_ref[g, kk] for g in range(A_GROUPS)], axis=1) for y_ref in (yr_ref, yi_ref)],
            axis=0).astype(BF16)
        zz = jnp.dot(m_ref[kk], ycat, preferred_element_type=F32)
        zr = zz[:n2].astype(BF16)
        zi = zz[n2:].astype(BF16)
        for g in range(A_GROUPS):
            cols = slice(g * GDIM, (g + 1) * GDIM)
            f = (jnp.dot(zr[:, cols], cs_ref[0], preferred_element_type=F32)
                 + jnp.dot(zi[:, cols], cs_ref[1], preferred_element_type=F32)) * scale
            f_ref[g, pl.ds(kk, n2, stride=SB_ROWS), :] = jnp.dot(
                f.astype(BF16), wm_ref[g], preferred_element_type=F32)
        return carry

    lax.fori_loop(0, SB_ROWS, body, 0)
    mixed = jnp.concatenate([f_ref[g] for g in range(A_GROUPS)], axis=1)
    gate = z_ref[...].reshape(n2 * SB_ROWS, CB).astype(F32)
    o_ref[...] = (mixed * gate).astype(BF16).reshape(n2, SB_ROWS, CB)


def _fourier(h_a, misc, f1_tab, m_tab, cs_tab, wmap_bf, batch, seq):
    n2 = B_CHUNK
    n1 = seq // n2
    x4 = h_a.reshape(batch, n1, n2, CB)
    xspec = lambda g: pl.BlockSpec((None, n1, SA_ROWS, LANES), lambda b, t: (b, 0, t, g))
    y = pl.pallas_call(
        _fft_a_kernel,
        grid=(batch, n2 // SA_ROWS),
        in_specs=[xspec(0), xspec(1), xspec(2), xspec(3),
                  pl.BlockSpec((2 * n1, n1), lambda b, t: (0, 0))],
        out_specs=pl.BlockSpec((None, A_GROUPS, 2 * n1, SA_ROWS, LANES), lambda b, t: (b, 0, 0, t, 0)),
        out_shape=jax.ShapeDtypeStruct((batch, A_GROUPS, 2 * n1, n2, LANES), F32),
        scratch_shapes=[pltpu.VMEM((A_GROUPS, n1 * SA_ROWS, LANES), F32),
                        pltpu.VMEM((A_GROUPS, 2 * n1 * SA_ROWS, LANES), F32)],
        compiler_params=_params("parallel", "parallel"),
        name="fft_a",
    )(x4, x4, x4, x4, f1_tab)
    nk = n1 // SB_ROWS
    z4 = misc.reshape(batch, n2, n1, N_MISC * CB)
    out = pl.pallas_call(
        functools.partial(_fft_b_kernel, scale=1.0 / math.sqrt(seq * GDIM)),
        grid=(nk, batch),
        in_specs=[pl.BlockSpec((None, A_GROUPS, SB_ROWS, n2, LANES), lambda k, b: (b, 0, k, 0, 0)),
                  pl.BlockSpec((None, A_GROUPS, SB_ROWS, n2, LANES), lambda k, b: (b, 0, nk + k, 0, 0)),
                  pl.BlockSpec((SB_ROWS, 2 * n2, 2 * n2), lambda k, b: (k, 0, 0)),
                  pl.BlockSpec((None, n2, SB_ROWS, CB), lambda k, b: (b, 0, k, MISC_AZ)),
                  pl.BlockSpec((2, GDIM, GDIM), lambda k, b: (0, 0, 0)),
                  pl.BlockSpec((A_GROUPS, GDIM, GDIM), lambda k, b: (0, 0, 0))],
        out_specs=pl.BlockSpec((None, n2, SB_ROWS, CB), lambda k, b: (b, 0, k, 0)),
        out_shape=jax.ShapeDtypeStruct((batch, n2, n1, CB), BF16),
        scratch_shapes=[pltpu.VMEM((A_GROUPS, n2 * SB_ROWS, LANES), F32)],
        compiler_params=_params("parallel", "parallel"),
        name="fft_b",
    )(y, y, m_tab, z4, cs_tab, wmap_bf)
    return out.reshape(batch * seq, CB)


def _dft_tables(seq):
    n2 = B_CHUNK
    n1 = seq // n2
    two_pi = 2.0 * math.pi

    def cs(idx, period):
        ang = (idx % period).astype(F32) * (two_pi / period)
        return jnp.cos(ang), jnp.sin(ang)

    i1 = jnp.arange(n1, dtype=jnp.int32)
    c1, s1 = cs(i1[:, None] * i1[None, :], n1)
    f1_tab = jnp.concatenate([c1, -s1], axis=0).astype(BF16)
    k = i1[:, None, None] + n1 * jnp.arange(n2, dtype=jnp.int32)[None, :, None]
    s2 = jnp.arange(n2, dtype=jnp.int32)[None, None, :]
    cm, sm = cs(k * s2, seq)
    m_tab = jnp.concatenate([jnp.concatenate([cm, sm], axis=2),
                             jnp.concatenate([-sm, cm], axis=2)], axis=1).astype(BF16)
    ic = jnp.arange(GDIM, dtype=jnp.int32)
    cc, sc = cs(ic[:, None] * ic[None, :], GDIM)
    cs_tab = jnp.stack([cc, sc]).astype(BF16)
    return f1_tab, m_tab, cs_tab


def _attn_kernel(q_ref, kc_ref, kp_ref, kn_ref, vc_ref, vp_ref, vn_ref, o_ref, l_ref, *, seq_len, tq):
    t = pl.program_id(2)
    tk = tq + 2 * HALF_WIN
    q = q_ref[...] * jnp.asarray(1.0 / math.sqrt(C_HEAD_DIM), BF16)
    k = jnp.concatenate([kp_ref[...], kc_ref[...], kn_ref[...]], axis=0)
    v = jnp.concatenate([vp_ref[...], vc_ref[...], vn_ref[...]], axis=0)
    row = lax.broadcasted_iota(jnp.int32, (tq, tk), 0)
    col = lax.broadcasted_iota(jnp.int32, (tq, tk), 1)
    kpos = t * tq - HALF_WIN + col
    valid = (jnp.abs(row + HALF_WIN - col) <= HALF_WIN) & (kpos >= 0) & (kpos < seq_len)
    lo = lax.broadcasted_iota(jnp.int32, (tq, LANES), 1) < C_HEAD_DIM
    for hp in range(C_HEADS // 2):
        cols = slice(hp * LANES, (hp + 1) * LANES)
        qp, kp, vp = q[:, cols], k[:, cols], v[:, cols]
        o_e, lse_e = [], []
        for e in range(2):
            sel = lo if e == 0 else jnp.logical_not(lo)
            qe = jnp.where(sel, qp, jnp.zeros_like(qp))
            s = lax.dot_general(qe, kp, (((1,), (1,)), ((), ())), preferred_element_type=F32)
            s = jnp.where(valid, s, NEG_BIG)
            m = jnp.max(s, axis=1, keepdims=True)
            p = jnp.exp(s - m)
            l = jnp.sum(p, axis=1, keepdims=True)
            pv = jnp.dot(p.astype(BF16), vp, preferred_element_type=F32)
            o_e.append(pv / l)
            lse_e.append(m + jnp.log(l))
        o_ref[:, cols] = jnp.where(lo, o_e[0], o_e[1]).astype(BF16)
        l_ref[:, cols] = jnp.where(lo, lse_e[0], lse_e[1])


def _attn_group(qkv, g):
    batch, dil, ln, _ = qkv.shape
    tq = min(128, ln)
    per = tq // HALF_WIN
    nhalo = ln // HALF_WIN

    def cur(blk):
        return pl.BlockSpec((None, None, tq, CB), lambda b, r, t: (b, r, t, blk))

    def before(blk):
        return pl.BlockSpec((None, None, HALF_WIN, CB),
                            lambda b, r, t: (b, r, jnp.maximum(t * per - 1, 0), blk))

    def after(blk):
        return pl.BlockSpec((None, None, HALF_WIN, CB),
                            lambda b, r, t: (b, r, jnp.minimum((t + 1) * per, nhalo - 1), blk))

    tile = pl.BlockSpec((None, None, tq, CB), lambda b, r, t: (b, r, t, 0))
    return pl.pallas_call(
        functools.partial(_attn_kernel, seq_len=ln, tq=tq),
        grid=(batch, dil, ln // tq),
        in_specs=[cur(0), cur(1), before(1), after(1), cur(2), before(2), after(2)],
        out_specs=[tile, tile],
        out_shape=[jax.ShapeDtypeStruct((batch, dil, ln, CB), BF16),
                   jax.ShapeDtypeStruct((batch, dil, ln, CB), F32)],
        compiler_params=_params("parallel", "parallel", "parallel"),
        name=f"attn{g}",
    )(*([qkv] * 7))


def _combine_kernel(o0_ref, l0_ref, o1_ref, l1_ref, o2_ref, l2_ref, cz_ref, y_ref, *scr):
    tm = o0_ref.shape[0]
    for (o_ref, l_ref), (os_ref, ls_ref), dil in zip(((o1_ref, l1_ref), (o2_ref, l2_ref)),
                                                      (scr[:2], scr[2:]), C_DILATIONS[1:]):
        n = tm // dil
        for r in range(dil):
            rows = pl.ds(r, n, stride=dil)
            for c in range(NSLAB):
                cols = slice(c * LANES, (c + 1) * LANES)
                os_ref[c, rows, :] = o_ref[r, :, cols].astype(F32)
                ls_ref[c, rows, :] = l_ref[r, :, cols]
    for c in range(NSLAB):
        cols = slice(c * LANES, (c + 1) * LANES)
        l0, l1, l2 = l0_ref[:, cols], scr[1][c], scr[3][c]
        mx = jnp.maximum(jnp.maximum(l0, l1), l2)
        e0, e1, e2 = jnp.exp(l0 - mx), jnp.exp(l1 - mx), jnp.exp(l2 - mx)
        num = e0 * o0_ref[:, cols].astype(F32) + e1 * scr[0][c] + e2 * scr[2][c]
        y_ref[:, cols] = (num / (e0 + e1 + e2) * cz_ref[:, cols].astype(F32)).astype(BF16)


def _combine(res, misc, batch, seq, tm):
    (o0, l0), (o1, l1), (o2, l2) = res
    nseq = seq // tm
    nat = pl.BlockSpec((None, None, tm, CB), lambda i: (i // nseq, 0, i % nseq, 0))

    def dil_spec(dil):
        return pl.BlockSpec((None, dil, tm // dil, CB), lambda i: (i // nseq, 0, i % nseq, 0))

    s1, s2 = dil_spec(C_DILATIONS[1]), dil_spec(C_DILATIONS[2])
    return pl.pallas_call(
        _combine_kernel,
        grid=(batch * nseq,),
        in_specs=[nat, nat, s1, s1, s2, s2, pl.BlockSpec((tm, CB), lambda i: (i, MISC_CZ))],
        out_specs=pl.BlockSpec((tm, CB), lambda i: (i, 0)),
        out_shape=jax.ShapeDtypeStruct((batch * seq, CB), BF16),
        scratch_shapes=[pltpu.VMEM((NSLAB, tm, LANES), F32)] * 4,
        compiler_params=_params("parallel"),
        name="combine",
    )(o0, l0, o1, l1, o2, l2, misc)


def _merge_kernel(x_ref, ya_ref, yb_ref, yc_ref, wg0_ref, wg1_ref, wg2_ref, wb_ref, o_ref, xb_ref):
    @pl.when(pl.program_id(1) == 0)
    def _():
        xb_ref[...] = x_ref[...].astype(BF16)

    xb = xb_ref[...]
    acc = None
    for kk, (y_ref, wg_ref) in enumerate(((ya_ref, wg0_ref), (yb_ref, wg1_ref), (yc_ref, wg2_ref))):
        gate = jax.nn.sigmoid(jnp.dot(xb, wg_ref[...], preferred_element_type=F32))
        term = gate * jnp.dot(y_ref[...], wb_ref[kk], preferred_element_type=F32)
        acc = term if acc is None else acc + term
    o_ref[...] = acc.astype(BF16)


def _merge(x2, ya, yb, yc, w_gate, wbr_bf, tm):
    m = x2.shape[0]
    ncol = D_MODEL // CB
    ybs = pl.BlockSpec((tm, CB), lambda i, c: (i, 0))

    def gate_spec(kk):
        return pl.BlockSpec((D_MODEL, CB), lambda i, c: (0, kk * ncol + c))

    return pl.pallas_call(
        _merge_kernel,
        grid=(m // tm, ncol),
        in_specs=[pl.BlockSpec((tm, D_MODEL), lambda i, c: (i, 0)), ybs, ybs, ybs,
                  gate_spec(0), gate_spec(1), gate_spec(2),
                  pl.BlockSpec((N_BRANCH, W_BR, CB), lambda i, c: (0, 0, c))],
        out_specs=pl.BlockSpec((tm, CB), lambda i, c: (i, c)),
        out_shape=jax.ShapeDtypeStruct((m, D_MODEL), BF16),
        scratch_shapes=[pltpu.VMEM((tm, D_MODEL), BF16)],
        compiler_params=_params("parallel", "arbitrary"),
        name="merge",
    )(x2, ya, yb, yc, w_gate, w_gate, w_gate, wbr_bf)


def _outproj_kernel(x_ref, m_ref, wo_ref, g_ref, b_ref, o_ref):
    out = jnp.dot(m_ref[...], wo_ref[...], preferred_element_type=F32)
    z = ALPHA * x_ref[...] + out
    mu = jnp.mean(z, axis=-1, keepdims=True)
    zc = z - mu
    var = jnp.mean(zc * zc, axis=-1, keepdims=True)
    o_ref[...] = zc * lax.rsqrt(var + LN_EPS) * g_ref[...] + b_ref[...]


def _outproj(x2, merged, wo_bf, ln_g, ln_b, tm):
    m = x2.shape[0]
    row = pl.BlockSpec((tm, D_MODEL), lambda i: (i, 0))
    vec = pl.BlockSpec((1, D_MODEL), lambda i: (0, 0))
    return pl.pallas_call(
        _outproj_kernel,
        grid=(m // tm,),
        in_specs=[row, row, pl.BlockSpec((D_MODEL, D_MODEL), lambda i: (0, 0)), vec, vec],
        out_specs=row,
        out_shape=jax.ShapeDtypeStruct((m, D_MODEL), F32),
        compiler_params=_params("parallel"),
        name="outproj",
    )(x2, merged, wo_bf, ln_g, ln_b)


def _rope_tables(seq):
    half = ROPE_DIM // 2
    pos = jnp.arange(seq, dtype=F32)
    inv_freq = ROPE_THETA ** (-jnp.arange(half, dtype=F32) / half)
    ang = pos[:, None] * inv_freq[None, :]
    cos, sin = jnp.cos(ang), jnp.sin(ang)
    ones = jnp.ones((seq, C_HEAD_DIM - ROPE_DIM), F32)
    zeros = jnp.zeros((seq, C_HEAD_DIM - half), F32)
    cos_h = jnp.concatenate([cos, cos, ones], axis=1)
    lo_h = jnp.concatenate([-sin, zeros], axis=1)
    hi_h = jnp.concatenate([jnp.zeros((seq, half), F32), sin, zeros[:, :C_HEAD_DIM - ROPE_DIM]], axis=1)
    rep = lambda t: jnp.tile(t, (1, LANES // C_HEAD_DIM))
    return rep(cos_h), rep(lo_h), rep(hi_h)


def _mix_columns(w):
    blk = lambda off: w[:, off:off + CB]
    cols = [blk(OFF_A), blk(OFF_AZ), blk(OFF_B), blk(OFF_B + CB), blk(OFF_BZ), blk(OFF_CZ)]
    for g in range(C_NG):
        cols += [blk(OFF_C + part * C_QKV + g * CB) for part in range(3)]
    return jnp.concatenate(cols, axis=1).astype(BF16), w[:, OFF_G:].astype(BF16)


def _layer(x2, batch, seq, wl, rope_t, dft_t):
    h_a, misc, g0, g1, g2 = _inproj(x2, wl["w_mix"], *rope_t, batch, seq, 1024)
    y_a = _fourier(h_a, misc, *dft_t, wl["w_amap"], batch, seq)
    y_b = _sgu(misc, wl["sgu_ln_g"], wl["sgu_ln_b"], wl["sgu_w"], wl["sgu_bias"], 512)
    groups = (g0.reshape(batch, 1, seq, 3 * CB), g1, g2)
    y_c = _combine([_attn_group(qkv, g) for g, qkv in enumerate(groups)], misc, batch, seq, 1024)
    merged = _merge(x2, y_a, y_b, y_c, wl["w_gate"], wl["w_branch"], 1024)
    return _outproj(x2, merged, wl["w_out"], wl["ln_g"], wl["ln_b"], 512)


def kernel(x_prompt, x_sample, w_in, w_amap, sgu_ln_g, sgu_ln_b, sgu_w, sgu_b, w_branch, w_out, ln_g, ln_b):
    layers = []
    for l in range(DEPTH):
        w_mix, w_gate = _mix_columns(w_in[l])
        layers.append({
            "w_mix": w_mix,
            "w_gate": w_gate,
            "w_amap": w_amap[l].astype(BF16),
            "sgu_ln_g": sgu_ln_g[l][None, :],
            "sgu_ln_b": sgu_ln_b[l][None, :],
            "sgu_w": sgu_w[l].astype(BF16),
            "sgu_bias": jnp.repeat(sgu_b[l].T, GDIM, axis=1),
            "w_branch": w_branch[l].astype(BF16),
            "w_out": w_out[l].astype(BF16),
            "ln_g": ln_g[l][None, :],
            "ln_b": ln_b[l][None, :],
        })
    outs = []
    for x in (x_prompt, x_sample):
        batch, seq, _ = x.shape
        rope_t = _rope_tables(seq)
        dft_t = _dft_tables(seq)
        x2 = x.reshape(batch * seq, D_MODEL)
        for wl in layers:
            x2 = _layer(x2, batch, seq, wl, rope_t, dft_t)
        outs.append(x2.reshape(batch, seq, D_MODEL))
    return tuple(outs)
```

```python
import functools
import math

import jax
import jax.numpy as jnp
from jax import lax
from jax.experimental import pallas as pl
from jax.experimental.pallas import tpu as pltpu

F32 = jnp.float32
BF16 = jnp.bfloat16

D_MODEL = 2048
DEPTH = 2
W_BR = 512
A_GROUPS = 4
GDIM = 128
B_CHUNK = 128
C_DILATIONS = (1, 4, 16)
C_NG = len(C_DILATIONS)
C_HEADS = 8
C_HEAD_DIM = 64
C_QKV = C_NG * C_HEADS * C_HEAD_DIM
ROPE_DIM = C_HEAD_DIM // 4
ROPE_THETA = 500000.0
N_BRANCH = 3
ALPHA = (2 * DEPTH) ** 0.25
LN_EPS = 1e-5
NEG_BIG = -1e30
HALF_WIN = 64

LANES = 128
BF16_ROWS = 16
CB = 512
NSLAB = CB // LANES
OFF_A, OFF_AZ, OFF_B, OFF_BZ, OFF_C = 0, 512, 1024, 2048, 2560
OFF_CZ = OFF_C + 3 * C_QKV
OFF_G = OFF_CZ + W_BR
N_MIX_BLOCKS = 15
MISC_AZ, MISC_U, MISC_V, MISC_BZ, MISC_CZ = 0, 1, 2, 3, 4
N_MISC = 5
J_MISC = 1
J_QKV = J_MISC + N_MISC

VMEM_LIMIT = 56 * 1024 * 1024


def _params(*sem):
    return pltpu.CompilerParams(dimension_semantics=sem, vmem_limit_bytes=VMEM_LIMIT)


def _slabs(x):
    return [x[:, c * LANES:(c + 1) * LANES] for c in range(x.shape[1] // LANES)]


def _silu(h):
    return h * jax.nn.sigmoid(h)


def _gelu_tanh(h):
    c = math.sqrt(2.0 / math.pi)
    return h * (0.5 * (1.0 + jnp.tanh(c * (h + 0.044715 * (h * h * h)))))


def _rope_slab(hc, cos, sin_lo, sin_hi):
    up = pltpu.roll(hc, LANES - ROPE_DIM // 2, axis=1)
    dn = pltpu.roll(hc, ROPE_DIM // 2, axis=1)
    return hc * cos + up * sin_lo + dn * sin_hi


ROW_CHUNK = 256


def _inproj_kernel(x_ref, w_ref, tab_ref, ha_ref, misc_ref, g0_ref, g1_ref, g2_ref, xb_ref, acc_ref):
    j = pl.program_id(1)
    tm = x_ref.shape[0]

    @pl.when(j == 0)
    def _():
        xb_ref[...] = x_ref[...].astype(BF16)

    def chunks():
        for rc in range(tm // ROW_CHUNK):
            rows = slice(rc * ROW_CHUNK, (rc + 1) * ROW_CHUNK)
            yield rc, rows, jnp.dot(xb_ref[rows, :], w_ref[...], preferred_element_type=F32)

    @pl.when(j == 0)
    def _():
        for _, rows, res in chunks():
            ha_ref[rows, :] = res

    @pl.when((j == J_MISC + MISC_AZ) | (j == J_MISC + MISC_BZ) | (j == J_MISC + MISC_CZ))
    def _():
        for _, rows, res in chunks():
            misc_ref[rows, :] = _silu(res).astype(BF16)

    @pl.when((j == J_MISC + MISC_U) | (j == J_MISC + MISC_V))
    def _():
        for _, rows, res in chunks():
            misc_ref[rows, :] = _gelu_tanh(res).astype(BF16)

    for g, (dil, out_ref) in enumerate(zip(C_DILATIONS, (g0_ref, g1_ref, g2_ref))):
        for part in range(3):
            @pl.when(j == J_QKV + 3 * g + part)
            def _(g=g, dil=dil, out_ref=out_ref, rotate=part < 2):
                n = ROW_CHUNK // dil
                for rc, rows, res in chunks():
                    if dil > 1:
                        for c, slab in enumerate(_slabs(res)):
                            acc_ref[rc % 2, c] = slab
                    for r in range(dil):
                        trows = pl.ds(r * (tm // dil) + rc * n, n)
                        tabs = [tab_ref[g, kind, trows, :] for kind in range(3)] if rotate else None
                        for c in range(NSLAB):
                            cols = slice(c * LANES, (c + 1) * LANES)
                            if dil > 1:
                                hc = acc_ref[rc % 2, c, pl.ds(r, n, stride=dil), :]
                            else:
                                hc = res[:, cols]
                            if rotate:
                                hc = _rope_slab(hc, *tabs)
                            if dil > 1:
                                out_ref[r, rc * n:(rc + 1) * n, cols] = hc.astype(BF16)
                            else:
                                out_ref[rows, cols] = hc.astype(BF16)


def _inproj(x2, w_mix, rope_tab, batch, seq, tm):
    m = x2.shape[0]
    nseq = seq // tm

    def part(j, first):
        return jnp.clip(j - first, 0, 2)

    def dil_spec(g):
        dil = C_DILATIONS[g]
        return pl.BlockSpec((None, dil, tm // dil, CB),
                            lambda i, j: (i // nseq, 0, i % nseq, part(j, J_QKV + 3 * g)))

    def dil_shape(g):
        dil = C_DILATIONS[g]
        return jax.ShapeDtypeStruct((batch, dil, seq // dil, 3 * CB), BF16)

    return pl.pallas_call(
        _inproj_kernel,
        grid=(m // tm, N_MIX_BLOCKS),
        in_specs=[
            pl.BlockSpec((tm, D_MODEL), lambda i, j: (i, 0)),
            pl.BlockSpec((D_MODEL, CB), lambda i, j: (0, j)),
            pl.BlockSpec((C_NG, 3, tm, LANES), lambda i, j: (0, 0, i % nseq, 0)),
        ],
        out_specs=[
            pl.BlockSpec((tm, CB), lambda i, j: (i, 0)),
            pl.BlockSpec((tm, CB), lambda i, j: (i, jnp.clip(j - J_MISC, 0, N_MISC - 1))),
            pl.BlockSpec((tm, CB), lambda i, j: (i, part(j, J_QKV))),
            dil_spec(1), dil_spec(2),
        ],
        out_shape=[
            jax.ShapeDtypeStruct((m, CB), F32),
            jax.ShapeDtypeStruct((m, N_MISC * CB), BF16),
            jax.ShapeDtypeStruct((m, 3 * CB), BF16),
            dil_shape(1), dil_shape(2),
        ],
        scratch_shapes=[pltpu.VMEM((tm, D_MODEL), BF16), pltpu.VMEM((2, NSLAB, ROW_CHUNK, LANES), F32)],
        compiler_params=_params("parallel", "arbitrary"),
        name="inproj",
    )(x2, w_mix, rope_tab)


def _sgu_kernel(u_ref, v_ref, z_ref, g_ref, b_ref, ws_ref, bs_ref, o_ref):
    tm = u_ref.shape[0]
    v = v_ref[...].astype(F32)
    mu = jnp.mean(v, axis=-1, keepdims=True)
    vc = v - mu
    var = jnp.mean(vc * vc, axis=-1, keepdims=True)
    vn = (vc * lax.rsqrt(var + LN_EPS) * g_ref[...] + b_ref[...]).astype(BF16)
    bias = bs_ref[...]
    for n in range(tm // B_CHUNK):
        rows = slice(n * B_CHUNK, (n + 1) * B_CHUNK)
        mixed = jnp.concatenate(
            [jnp.dot(ws_ref[g], vn[rows, g * GDIM:(g + 1) * GDIM], preferred_element_type=F32)
             for g in range(A_GROUPS)], axis=1) + bias
        o_ref[rows, :] = (u_ref[rows, :].astype(F32) * mixed * z_ref[rows, :].astype(F32)).astype(BF16)


def _sgu(misc, ln_g, ln_b, ws_bf, bias_full, tm):
    m = misc.shape[0]
    col = lambda blk: pl.BlockSpec((tm, CB), lambda i: (i, blk))
    full = lambda shape: pl.BlockSpec(shape, lambda i: (0,) * len(shape))
    return pl.pallas_call(
        _sgu_kernel,
        grid=(m // tm,),
        in_specs=[col(MISC_U), col(MISC_V), col(MISC_BZ), full((1, CB)), full((1, CB)),
                  full((A_GROUPS, B_CHUNK, B_CHUNK)), full((B_CHUNK, CB))],
        out_specs=pl.BlockSpec((tm, CB), lambda i: (i, 0)),
        out_shape=jax.ShapeDtypeStruct((m, CB), BF16),
        compiler_params=_params("parallel"),
        name="sgu",
    )(misc, misc, misc, ln_g, ln_b, ws_bf, bias_full)


SA_ROWS = 8
SB_ROWS = BF16_ROWS


def _fft_a_kernel(x0_ref, x1_ref, x2_ref, x3_ref, f_ref, y_ref, xin_ref, yout_ref):
    n1 = x0_ref.shape[0]
    for g, x_ref in enumerate((x0_ref, x1_ref, x2_ref, x3_ref)):
        xin_ref[g] = x_ref[...].reshape(n1 * SA_ROWS, LANES)
    for jj in range(SA_ROWS):
        xj = jnp.concatenate([xin_ref[g, pl.ds(jj, n1, stride=SA_ROWS), :] for g in range(A_GROUPS)],
                             axis=1).astype(BF16)
        yj = jnp.dot(f_ref[...], xj, preferred_element_type=F32)
        for g, slab in enumerate(_slabs(yj)):
            yout_ref[g, pl.ds(jj, 2 * n1, stride=SA_ROWS), :] = slab
    for g in range(A_GROUPS):
        y_ref[g] = yout_ref[g].reshape(2 * n1, SA_ROWS, LANES)


def _fft_b_kernel(yr_ref, yi_ref, m_ref, z_ref, cs_ref, wm_ref, o_ref, f_ref, *, scale):
    n2 = yr_ref.shape[2]

    def body(kk, carry):
        ycat = jnp.concatenate(
            [jnp.concatenate([y_ref[g, kk] for g in range(A_GROUPS)], axis=1) for y_ref in (yr_ref, yi_ref)],
            axis=0).astype(BF16)
        zz = jnp.dot(m_ref[kk], ycat, preferred_element_type=F32)
        zr = zz[:n2].astype(BF16)
        zi = zz[n2:].astype(BF16)
        for g in range(A_GROUPS):
            cols = slice(g * GDIM, (g + 1) * GDIM)
            f = (jnp.dot(zr[:, cols], cs_ref[0], preferred_element_type=F32)
                 + jnp.dot(zi[:, cols], cs_ref[1], preferred_element_type=F32)) * scale
            f_ref[g, pl.ds(kk, n2, stride=SB_ROWS), :] = jnp.dot(
                f.astype(BF16), wm_ref[g], preferred_element_type=F32)
        return carry

    lax.fori_loop(0, SB_ROWS, body, 0)
    mixed = jnp.concatenate([f_ref[g] for g in range(A_GROUPS)], axis=1)
    gate = z_ref[...].reshape(n2 * SB_ROWS, CB).astype(F32)
    o_ref[...] = (mixed * gate).astype(BF16).reshape(n2, SB_ROWS, CB)


def _fourier(h_a, misc, f1_tab, m_tab, cs_tab, wmap_bf, batch, seq):
    n2 = B_CHUNK
    n1 = seq // n2
    x4 = h_a.reshape(batch, n1, n2, CB)
    xspec = lambda g: pl.BlockSpec((None, n1, SA_ROWS, LANES), lambda b, t: (b, 0, t, g))
    y = pl.pallas_call(
        _fft_a_kernel,
        grid=(batch, n2 // SA_ROWS),
        in_specs=[xspec(0), xspec(1), xspec(2), xspec(3),
                  pl.BlockSpec((2 * n1, n1), lambda b, t: (0, 0))],
        out_specs=pl.BlockSpec((None, A_GROUPS, 2 * n1, SA_ROWS, LANES), lambda b, t: (b, 0, 0, t, 0)),
        out_shape=jax.ShapeDtypeStruct((batch, A_GROUPS, 2 * n1, n2, LANES), F32),
        scratch_shapes=[pltpu.VMEM((A_GROUPS, n1 * SA_ROWS, LANES), F32),
                        pltpu.VMEM((A_GROUPS, 2 * n1 * SA_ROWS, LANES), F32)],
        compiler_params=_params("parallel", "parallel"),
        name="fft_a",
    )(x4, x4, x4, x4, f1_tab)
    nk = n1 // SB_ROWS
    z4 = misc.reshape(batch, n2, n1, N_MISC * CB)
    out = pl.pallas_call(
        functools.partial(_fft_b_kernel, scale=1.0 / math.sqrt(seq * GDIM)),
        grid=(nk, batch),
        in_specs=[pl.BlockSpec((None, A_GROUPS, SB_ROWS, n2, LANES), lambda k, b: (b, 0, k, 0, 0)),
                  pl.BlockSpec((None, A_GROUPS, SB_ROWS, n2, LANES), lambda k, b: (b, 0, nk + k, 0, 0)),
                  pl.BlockSpec((SB_ROWS, 2 * n2, 2 * n2), lambda k, b: (k, 0, 0)),
                  pl.BlockSpec((None, n2, SB_ROWS, CB), lambda k, b: (b, 0, k, MISC_AZ)),
                  pl.BlockSpec((2, GDIM, GDIM), lambda k, b: (0, 0, 0)),
                  pl.BlockSpec((A_GROUPS, GDIM, GDIM), lambda k, b: (0, 0, 0))],
        out_specs=pl.BlockSpec((None, n2, SB_ROWS, CB), lambda k, b: (b, 0, k, 0)),
        out_shape=jax.ShapeDtypeStruct((batch, n2, n1, CB), BF16),
        scratch_shapes=[pltpu.VMEM((A_GROUPS, n2 * SB_ROWS, LANES), F32)],
        compiler_params=_params("parallel", "parallel"),
        name="fft_b",
    )(y, y, m_tab, z4, cs_tab, wmap_bf)
    return out.reshape(batch * seq, CB)


def _dft_tables(seq):
    n2 = B_CHUNK
    n1 = seq // n2
    two_pi = 2.0 * math.pi

    def cs(idx, period):
        ang = (idx % period).astype(F32) * (two_pi / period)
        return jnp.cos(ang), jnp.sin(ang)

    i1 = jnp.arange(n1, dtype=jnp.int32)
    c1, s1 = cs(i1[:, None] * i1[None, :], n1)
    f1_tab = jnp.concatenate([c1, -s1], axis=0).astype(BF16)
    k = i1[:, None, None] + n1 * jnp.arange(n2, dtype=jnp.int32)[None, :, None]
    s2 = jnp.arange(n2, dtype=jnp.int32)[None, None, :]
    cm, sm = cs(k * s2, seq)
    m_tab = jnp.concatenate([jnp.concatenate([cm, sm], axis=2),
                             jnp.concatenate([-sm, cm], axis=2)], axis=1).astype(BF16)
    ic = jnp.arange(GDIM, dtype=jnp.int32)
    cc, sc = cs(ic[:, None] * ic[None, :], GDIM)
    cs_tab = jnp.stack([cc, sc]).astype(BF16)
    return f1_tab, m_tab, cs_tab


def _attn_kernel(q_ref, kc_ref, kp_ref, kn_ref, vc_ref, vp_ref, vn_ref, o_ref, l_ref, *, seq_len, tq):
    t = pl.program_id(2)
    tk = tq + 2 * HALF_WIN
    q = q_ref[...] * jnp.asarray(1.0 / math.sqrt(C_HEAD_DIM), BF16)
    k = jnp.concatenate([kp_ref[...], kc_ref[...], kn_ref[...]], axis=0)
    v = jnp.concatenate([vp_ref[...], vc_ref[...], vn_ref[...]], axis=0)
    row = lax.broadcasted_iota(jnp.int32, (tq, tk), 0)
    col = lax.broadcasted_iota(jnp.int32, (tq, tk), 1)
    kpos = t * tq - HALF_WIN + col
    valid = (jnp.abs(row + HALF_WIN - col) <= HALF_WIN) & (kpos >= 0) & (kpos < seq_len)
    lo = lax.broadcasted_iota(jnp.int32, (tq, LANES), 1) < C_HEAD_DIM
    for hp in range(C_HEADS // 2):
        cols = slice(hp * LANES, (hp + 1) * LANES)
        qp, kp, vp = q[:, cols], k[:, cols], v[:, cols]
        o_e, lse_e = [], []
        for e in range(2):
            sel = lo if e == 0 else jnp.logical_not(lo)
            qe = jnp.where(sel, qp, jnp.zeros_like(qp))
            s = lax.dot_general(qe, kp, (((1,), (1,)), ((), ())), preferred_element_type=F32)
            s = jnp.where(valid, s, NEG_BIG)
            m = jnp.max(s, axis=1, keepdims=True)
            p = jnp.exp(s - m)
            l = jnp.sum(p, axis=1, keepdims=True)
            pv = jnp.dot(p.astype(BF16), vp, preferred_element_type=F32)
            o_e.append(pv / l)
            lse_e.append(m + jnp.log(l))
        o_ref[:, cols] = jnp.where(lo, o_e[0], o_e[1]).astype(BF16)
        l_ref[:, cols] = jnp.where(lo, lse_e[0], lse_e[1])


def _attn_group(qkv, g):
    batch, dil, ln, _ = qkv.shape
    tq = min(128, ln)
    per = tq // HALF_WIN
    nhalo = ln // HALF_WIN

    def cur(blk):
        return pl.BlockSpec((None, None, tq, CB), lambda b, r, t: (b, r, t, blk))

    def before(blk):
        return pl.BlockSpec((None, None, HALF_WIN, CB),
                            lambda b, r, t: (b, r, jnp.maximum(t * per - 1, 0), blk))

    def after(blk):
        return pl.BlockSpec((None, None, HALF_WIN, CB),
                            lambda b, r, t: (b, r, jnp.minimum((t + 1) * per, nhalo - 1), blk))

    tile = pl.BlockSpec((None, None, tq, CB), lambda b, r, t: (b, r, t, 0))
    return pl.pallas_call(
        functools.partial(_attn_kernel, seq_len=ln, tq=tq),
        grid=(batch, dil, ln // tq),
        in_specs=[cur(0), cur(1), before(1), after(1), cur(2), before(2), after(2)],
        out_specs=[tile, tile],
        out_shape=[jax.ShapeDtypeStruct((batch, dil, ln, CB), BF16),
                   jax.ShapeDtypeStruct((batch, dil, ln, CB), F32)],
        compiler_params=_params("parallel", "parallel", "parallel"),
        name=f"attn{g}",
    )(*([qkv] * 7))


def _combine_kernel(o0_ref, l0_ref, o1_ref, l1_ref, o2_ref, l2_ref, cz_ref, y_ref, *scr):
    tm = o0_ref.shape[0]
    for (o_ref, l_ref), (os_ref, ls_ref), dil in zip(((o1_ref, l1_ref), (o2_ref, l2_ref)),
                                                      (scr[:2], scr[2:]), C_DILATIONS[1:]):
        n = tm // dil
        for r in range(dil):
            rows = pl.ds(r, n, stride=dil)
            for c in range(NSLAB):
                cols = slice(c * LANES, (c + 1) * LANES)
                os_ref[c, rows, :] = o_ref[r, :, cols].astype(F32)
                ls_ref[c, rows, :] = l_ref[r, :, cols]
    for c in range(NSLAB):
        cols = slice(c * LANES, (c + 1) * LANES)
        l0, l1, l2 = l0_ref[:, cols], scr[1][c], scr[3][c]
        mx = jnp.maximum(jnp.maximum(l0, l1), l2)
        e0, e1, e2 = jnp.exp(l0 - mx), jnp.exp(l1 - mx), jnp.exp(l2 - mx)
        num = e0 * o0_ref[:, cols].astype(F32) + e1 * scr[0][c] + e2 * scr[2][c]
        y_ref[:, cols] = (num / (e0 + e1 + e2) * cz_ref[:, cols].astype(F32)).astype(BF16)


def _combine(res, misc, batch, seq, tm):
    (o0, l0), (o1, l1), (o2, l2) = res
    nseq = seq // tm
    nat = pl.BlockSpec((None, None, tm, CB), lambda i: (i // nseq, 0, i % nseq, 0))

    def dil_spec(dil):
        return pl.BlockSpec((None, dil, tm // dil, CB), lambda i: (i // nseq, 0, i % nseq, 0))

    s1, s2 = dil_spec(C_DILATIONS[1]), dil_spec(C_DILATIONS[2])
    return pl.pallas_call(
        _combine_kernel,
        grid=(batch * nseq,),
        in_specs=[nat, nat, s1, s1, s2, s2, pl.BlockSpec((tm, CB), lambda i: (i, MISC_CZ))],
        out_specs=pl.BlockSpec((tm, CB), lambda i: (i, 0)),
        out_shape=jax.ShapeDtypeStruct((batch * seq, CB), BF16),
        scratch_shapes=[pltpu.VMEM((NSLAB, tm, LANES), F32)] * 4,
        compiler_params=_params("parallel"),
        name="combine",
    )(o0, l0, o1, l1, o2, l2, misc)


def _merge_kernel(x_ref, ya_ref, yb_ref, yc_ref, wg0_ref, wg1_ref, wg2_ref, wb_ref, o_ref, xb_ref):
    @pl.when(pl.program_id(1) == 0)
    def _():
        xb_ref[...] = x_ref[...].astype(BF16)

    xb = xb_ref[...]
    acc = None
    for kk, (y_ref, wg_ref) in enumerate(((ya_ref, wg0_ref), (yb_ref, wg1_ref), (yc_ref, wg2_ref))):
        gate = jax.nn.sigmoid(jnp.dot(xb, wg_ref[...], preferred_element_type=F32))
        term = gate * jnp.dot(y_ref[...], wb_ref[kk], preferred_element_type=F32)
        acc = term if acc is None else acc + term
    o_ref[...] = acc.astype(BF16)


def _merge(x2, ya, yb, yc, w_gate, wbr_bf, tm):
    m = x2.shape[0]
    ncol = D_MODEL // CB
    ybs = pl.BlockSpec((tm, CB), lambda i, c: (i, 0))

    def gate_spec(kk):
        return pl.BlockSpec((D_MODEL, CB), lambda i, c: (0, kk * ncol + c))

    return pl.pallas_call(
        _merge_kernel,
        grid=(m // tm, ncol),
        in_specs=[pl.BlockSpec((tm, D_MODEL), lambda i, c: (i, 0)), ybs, ybs, ybs,
                  gate_spec(0), gate_spec(1), gate_spec(2),
                  pl.BlockSpec((N_BRANCH, W_BR, CB), lambda i, c: (0, 0, c))],
        out_specs=pl.BlockSpec((tm, CB), lambda i, c: (i, c)),
        out_shape=jax.ShapeDtypeStruct((m, D_MODEL), BF16),
        scratch_shapes=[pltpu.VMEM((tm, D_MODEL), BF16)],
        compiler_params=_params("parallel", "arbitrary"),
        name="merge",
    )(x2, ya, yb, yc, w_gate, w_gate, w_gate, wbr_bf)


def _outproj_kernel(x_ref, m_ref, wo_ref, g_ref, b_ref, o_ref):
    out = jnp.dot(m_ref[...], wo_ref[...], preferred_element_type=F32)
    z = ALPHA * x_ref[...] + out
    mu = jnp.mean(z, axis=-1, keepdims=True)
    zc = z - mu
    var = jnp.mean(zc * zc, axis=-1, keepdims=True)
    o_ref[...] = zc * lax.rsqrt(var + LN_EPS) * g_ref[...] + b_ref[...]


def _outproj(x2, merged, wo_bf, ln_g, ln_b, tm):
    m = x2.shape[0]
    row = pl.BlockSpec((tm, D_MODEL), lambda i: (i, 0))
    vec = pl.BlockSpec((1, D_MODEL), lambda i: (0, 0))
    return pl.pallas_call(
        _outproj_kernel,
        grid=(m // tm,),
        in_specs=[row, row, pl.BlockSpec((D_MODEL, D_MODEL), lambda i: (0, 0)), vec, vec],
        out_specs=row,
        out_shape=jax.ShapeDtypeStruct((m, D_MODEL), F32),
        compiler_params=_params("parallel"),
        name="outproj",
    )(x2, merged, wo_bf, ln_g, ln_b)


def _rope_tables(seq, tm):
    half = ROPE_DIM // 2
    pos = jnp.arange(seq, dtype=F32)
    inv_freq = ROPE_THETA ** (-jnp.arange(half, dtype=F32) / half)
    ang = pos[:, None] * inv_freq[None, :]
    cos, sin = jnp.cos(ang), jnp.sin(ang)
    ones = jnp.ones((seq, C_HEAD_DIM - ROPE_DIM), F32)
    zeros = jnp.zeros((seq, C_HEAD_DIM - half), F32)
    cos_h = jnp.concatenate([cos, cos, ones], axis=1)
    lo_h = jnp.concatenate([-sin, zeros], axis=1)
    hi_h = jnp.concatenate([jnp.zeros((seq, half), F32), sin, zeros[:, :C_HEAD_DIM - ROPE_DIM]], axis=1)
    tabs = jnp.stack([jnp.tile(t, (1, LANES // C_HEAD_DIM)) for t in (cos_h, lo_h, hi_h)])

    def residue_major(dil):
        t = tabs.reshape(3, seq // tm, tm // dil, dil, LANES)
        return jnp.swapaxes(t, 2, 3).reshape(3, seq, LANES)

    return jnp.stack([residue_major(dil) for dil in C_DILATIONS])


def _mix_columns(w):
    blk = lambda off: w[:, off:off + CB]
    cols = [blk(OFF_A), blk(OFF_AZ), blk(OFF_B), blk(OFF_B + CB), blk(OFF_BZ), blk(OFF_CZ)]
    for g in range(C_NG):
        cols += [blk(OFF_C + part * C_QKV + g * CB) for part in range(3)]
    return jnp.concatenate(cols, axis=1).astype(BF16), w[:, OFF_G:].astype(BF16)


TM_INPROJ = 1024
TM_SGU = 512
TM_COMBINE = 1024
TM_MERGE = 1024
TM_OUTPROJ = 512


def _layer(x2, batch, seq, wl, rope_tab, dft_t):
    h_a, misc, g0, g1, g2 = _inproj(x2, wl["w_mix"], rope_tab, batch, seq, TM_INPROJ)
    y_a = _fourier(h_a, misc, *dft_t, wl["w_amap"], batch, seq)
    y_b = _sgu(misc, wl["sgu_ln_g"], wl["sgu_ln_b"], wl["sgu_w"], wl["sgu_bias"], TM_SGU)
    groups = (g0.reshape(batch, 1, seq, 3 * CB), g1, g2)
    y_c = _combine([_attn_group(qkv, g) for g, qkv in enumerate(groups)], misc, batch, seq, TM_COMBINE)
    merged = _merge(x2, y_a, y_b, y_c, wl["w_gate"], wl["w_branch"], TM_MERGE)
    return _outproj(x2, merged, wl["w_out"], wl["ln_g"], wl["ln_b"], TM_OUTPROJ)


def kernel(x_prompt, x_sample, w_in, w_amap, sgu_ln_g, sgu_ln_b, sgu_w, sgu_b, w_branch, w_out, ln_g, ln_b):
    layers = []
    for l in range(DEPTH):
        w_mix, w_gate = _mix_columns(w_in[l])
        layers.append({
            "w_mix": w_mix,
            "w_gate": w_gate,
            "w_amap": w_amap[l].astype(BF16),
            "sgu_ln_g": sgu_ln_g[l][None, :],
            "sgu_ln_b": sgu_ln_b[l][None, :],
            "sgu_w": sgu_w[l].astype(BF16),
            "sgu_bias": jnp.repeat(sgu_b[l].T, GDIM, axis=1),
            "w_branch": w_branch[l].astype(BF16),
            "w_out": w_out[l].astype(BF16),
            "ln_g": ln_g[l][None, :],
            "ln_b": ln_b[l][None, :],
        })
    outs = []
    for x in (x_prompt, x_sample):
        batch, seq, _ = x.shape
        rope_t = _rope_tables(seq, TM_INPROJ)
        dft_t = _dft_tables(seq)
        x2 = x.reshape(batch * seq, D_MODEL)
        for wl in layers:
            x2 = _layer(x2, batch, seq, wl, rope_t, dft_t)
        outs.append(x2.reshape(batch, seq, D_MODEL))
    return tuple(outs)
```

```python
import functools
import math

import jax
import jax.numpy as jnp
from jax import lax
from jax.experimental import pallas as pl
from jax.experimental.pallas import tpu as pltpu

F32 = jnp.float32
BF16 = jnp.bfloat16

D_MODEL = 2048
DEPTH = 2
W_BR = 512
A_GROUPS = 4
GDIM = 128
B_CHUNK = 128
C_DILATIONS = (1, 4, 16)
C_NG = len(C_DILATIONS)
C_HEADS = 8
C_HEAD_DIM = 64
C_QKV = C_NG * C_HEADS * C_HEAD_DIM
ROPE_DIM = C_HEAD_DIM // 4
ROPE_THETA = 500000.0
N_BRANCH = 3
ALPHA = (2 * DEPTH) ** 0.25
LN_EPS = 1e-5
NEG_BIG = -1e30
HALF_WIN = 64
Q_SCALE = math.log2(math.e) / math.sqrt(C_HEAD_DIM)

LANES = 128
BF16_ROWS = 16
CB = 512
NSLAB = CB // LANES
OFF_A, OFF_AZ, OFF_B, OFF_BZ, OFF_C = 0, 512, 1024, 2048, 2560
OFF_CZ = OFF_C + 3 * C_QKV
OFF_G = OFF_CZ + W_BR
N_MIX_BLOCKS = 15
MISC_AZ, MISC_U, MISC_V, MISC_BZ, MISC_CZ = 0, 1, 2, 3, 4
N_MISC = 5
J_MISC = 1
J_QKV = J_MISC + N_MISC

VMEM_LIMIT = 56 * 1024 * 1024


def _params(*sem):
    return pltpu.CompilerParams(dimension_semantics=sem, vmem_limit_bytes=VMEM_LIMIT)


def _slabs(x):
    return [x[:, c * LANES:(c + 1) * LANES] for c in range(x.shape[1] // LANES)]


def _silu(h):
    return h * jax.nn.sigmoid(h)


def _gelu_tanh(h):
    c = math.sqrt(2.0 / math.pi)
    return h * (0.5 * (1.0 + jnp.tanh(c * (h + 0.044715 * (h * h * h)))))


def _rope_slab(hc, cos, sin_lo, sin_hi):
    up = pltpu.roll(hc, LANES - ROPE_DIM // 2, axis=1)
    dn = pltpu.roll(hc, ROPE_DIM // 2, axis=1)
    return hc * cos + up * sin_lo + dn * sin_hi


ROW_CHUNK = 256


def _inproj_kernel(x_ref, w_ref, tab_ref, ha_ref, misc_ref, g0_ref, g1_ref, g2_ref, xb_ref, acc_ref):
    j = pl.program_id(1)
    tm = x_ref.shape[0]

    @pl.when(j == 0)
    def _():
        xb_ref[...] = x_ref[...].astype(BF16)

    def chunks():
        for rc in range(tm // ROW_CHUNK):
            rows = slice(rc * ROW_CHUNK, (rc + 1) * ROW_CHUNK)
            yield rc, rows, jnp.dot(xb_ref[rows, :], w_ref[...], preferred_element_type=F32)

    @pl.when(j == 0)
    def _():
        for _, rows, res in chunks():
            ha_ref[rows, :] = res

    @pl.when((j == J_MISC + MISC_AZ) | (j == J_MISC + MISC_BZ) | (j == J_MISC + MISC_CZ))
    def _():
        for _, rows, res in chunks():
            misc_ref[rows, :] = _silu(res).astype(BF16)

    @pl.when((j == J_MISC + MISC_U) | (j == J_MISC + MISC_V))
    def _():
        for _, rows, res in chunks():
            misc_ref[rows, :] = _gelu_tanh(res).astype(BF16)

    for g, (dil, out_ref) in enumerate(zip(C_DILATIONS, (g0_ref, g1_ref, g2_ref))):
        for part in range(3):
            @pl.when(j == J_QKV + 3 * g + part)
            def _(g=g, dil=dil, out_ref=out_ref, rotate=part < 2, is_q=part == 0):
                n = ROW_CHUNK // dil
                for rc, rows, res in chunks():
                    if dil > 1:
                        for c, slab in enumerate(_slabs(res)):
                            acc_ref[rc % 2, c] = slab
                    for r in range(dil):
                        trows = pl.ds(r * (tm // dil) + rc * n, n)
                        tabs = [tab_ref[g, kind, trows, :] for kind in range(3)] if rotate else None
                        for c in range(NSLAB):
                            cols = slice(c * LANES, (c + 1) * LANES)
                            if dil > 1:
                                hc = acc_ref[rc % 2, c, pl.ds(r, n, stride=dil), :]
                            else:
                                hc = res[:, cols]
                            if rotate:
                                hc = _rope_slab(hc, *tabs)
                            if is_q:
                                hc = hc * Q_SCALE
                            if dil > 1:
                                out_ref[r, rc * n:(rc + 1) * n, cols] = hc.astype(BF16)
                            else:
                                out_ref[rows, cols] = hc.astype(BF16)


def _inproj(x2, w_mix, rope_tab, batch, seq, tm):
    m = x2.shape[0]
    nseq = seq // tm

    def part(j, first):
        return jnp.clip(j - first, 0, 2)

    def dil_spec(g):
        dil = C_DILATIONS[g]
        return pl.BlockSpec((None, dil, tm // dil, CB),
                            lambda i, j: (i // nseq, 0, i % nseq, part(j, J_QKV + 3 * g)))

    def dil_shape(g):
        dil = C_DILATIONS[g]
        return jax.ShapeDtypeStruct((batch, dil, seq // dil, 3 * CB), BF16)

    return pl.pallas_call(
        _inproj_kernel,
        grid=(m // tm, N_MIX_BLOCKS),
        in_specs=[
            pl.BlockSpec((tm, D_MODEL), lambda i, j: (i, 0)),
            pl.BlockSpec((D_MODEL, CB), lambda i, j: (0, j)),
            pl.BlockSpec((C_NG, 3, tm, LANES), lambda i, j: (0, 0, i % nseq, 0)),
        ],
        out_specs=[
            pl.BlockSpec((tm, CB), lambda i, j: (i, 0)),
            pl.BlockSpec((tm, CB), lambda i, j: (i, jnp.clip(j - J_MISC, 0, N_MISC - 1))),
            pl.BlockSpec((tm, CB), lambda i, j: (i, part(j, J_QKV))),
            dil_spec(1), dil_spec(2),
        ],
        out_shape=[
            jax.ShapeDtypeStruct((m, CB), F32),
            jax.ShapeDtypeStruct((m, N_MISC * CB), BF16),
            jax.ShapeDtypeStruct((m, 3 * CB), BF16),
            dil_shape(1), dil_shape(2),
        ],
        scratch_shapes=[pltpu.VMEM((tm, D_MODEL), BF16), pltpu.VMEM((2, NSLAB, ROW_CHUNK, LANES), F32)],
        compiler_params=_params("parallel", "arbitrary"),
        name="inproj",
    )(x2, w_mix, rope_tab)


def _sgu_kernel(u_ref, v_ref, z_ref, g_ref, b_ref, ws_ref, bs_ref, o_ref):
    tm = u_ref.shape[0]
    v = v_ref[...].astype(F32)
    mu = jnp.mean(v, axis=-1, keepdims=True)
    vc = v - mu
    var = jnp.mean(vc * vc, axis=-1, keepdims=True)
    vn = (vc * lax.rsqrt(var + LN_EPS) * g_ref[...] + b_ref[...]).astype(BF16)
    bias = bs_ref[...]
    for n in range(tm // B_CHUNK):
        rows = slice(n * B_CHUNK, (n + 1) * B_CHUNK)
        mixed = jnp.concatenate(
            [jnp.dot(ws_ref[g], vn[rows, g * GDIM:(g + 1) * GDIM], preferred_element_type=F32)
             for g in range(A_GROUPS)], axis=1) + bias
        o_ref[rows, :] = (u_ref[rows, :].astype(F32) * mixed * z_ref[rows, :].astype(F32)).astype(BF16)


def _sgu(misc, ln_g, ln_b, ws_bf, bias_full, tm):
    m = misc.shape[0]
    col = lambda blk: pl.BlockSpec((tm, CB), lambda i: (i, blk))
    full = lambda shape: pl.BlockSpec(shape, lambda i: (0,) * len(shape))
    return pl.pallas_call(
        _sgu_kernel,
        grid=(m // tm,),
        in_specs=[col(MISC_U), col(MISC_V), col(MISC_BZ), full((1, CB)), full((1, CB)),
                  full((A_GROUPS, B_CHUNK, B_CHUNK)), full((B_CHUNK, CB))],
        out_specs=pl.BlockSpec((tm, CB), lambda i: (i, 0)),
        out_shape=jax.ShapeDtypeStruct((m, CB), BF16),
        compiler_params=_params("parallel"),
        name="sgu",
    )(misc, misc, misc, ln_g, ln_b, ws_bf, bias_full)


SA_ROWS = 8
SB_ROWS = BF16_ROWS


def _fft_a_kernel(x0_ref, x1_ref, x2_ref, x3_ref, f_ref, y_ref, xin_ref, yout_ref):
    n1 = x0_ref.shape[0]
    for g, x_ref in enumerate((x0_ref, x1_ref, x2_ref, x3_ref)):
        xin_ref[g] = x_ref[...].reshape(n1 * SA_ROWS, LANES)
    for jj in range(SA_ROWS):
        xj = jnp.concatenate([xin_ref[g, pl.ds(jj, n1, stride=SA_ROWS), :] for g in range(A_GROUPS)],
                             axis=1).astype(BF16)
        yj = jnp.dot(f_ref[...], xj, preferred_element_type=F32)
        for g, slab in enumerate(_slabs(yj)):
            yout_ref[g, pl.ds(jj, 2 * n1, stride=SA_ROWS), :] = slab
    for g in range(A_GROUPS):
        y_ref[g] = yout_ref[g].reshape(2 * n1, SA_ROWS, LANES)


def _fft_b_kernel(yr_ref, yi_ref, m_ref, z_ref, cs_ref, wm_ref, o_ref, f_ref, csw_ref, *, scale):
    n2 = yr_ref.shape[2]
    for g in range(A_GROUPS):
        csw_ref[g] = (jnp.dot(cs_ref[...], wm_ref[g], preferred_element_type=F32) * scale).astype(BF16)

    def body(kk, carry):
        ycat = jnp.concatenate(
            [jnp.concatenate([y_ref[g, kk] for g in range(A_GROUPS)], axis=1) for y_ref in (yr_ref, yi_ref)],
            axis=0).astype(BF16)
        zz = jnp.dot(m_ref[kk], ycat, preferred_element_type=F32).astype(BF16)
        for g in range(A_GROUPS):
            cols = slice(g * GDIM, (g + 1) * GDIM)
            z_ri = jnp.concatenate([zz[:n2, cols], zz[n2:, cols]], axis=1)
            f_ref[g, pl.ds(kk, n2, stride=SB_ROWS), :] = jnp.dot(
                z_ri, csw_ref[g], preferred_element_type=F32)
        return carry

    lax.fori_loop(0, SB_ROWS, body, 0)
    mixed = jnp.concatenate([f_ref[g] for g in range(A_GROUPS)], axis=1)
    gate = z_ref[...].reshape(n2 * SB_ROWS, CB).astype(F32)
    o_ref[...] = (mixed * gate).astype(BF16).reshape(n2, SB_ROWS, CB)


def _fourier(h_a, misc, f1_tab, m_tab, cs_tab, wmap_bf, batch, seq):
    n2 = B_CHUNK
    n1 = seq // n2
    x4 = h_a.reshape(batch, n1, n2, CB)
    xspec = lambda g: pl.BlockSpec((None, n1, SA_ROWS, LANES), lambda b, t: (b, 0, t, g))
    y = pl.pallas_call(
        _fft_a_kernel,
        grid=(batch, n2 // SA_ROWS),
        in_specs=[xspec(0), xspec(1), xspec(2), xspec(3),
                  pl.BlockSpec((2 * n1, n1), lambda b, t: (0, 0))],
        out_specs=pl.BlockSpec((None, A_GROUPS, 2 * n1, SA_ROWS, LANES), lambda b, t: (b, 0, 0, t, 0)),
        out_shape=jax.ShapeDtypeStruct((batch, A_GROUPS, 2 * n1, n2, LANES), F32),
        scratch_shapes=[pltpu.VMEM((A_GROUPS, n1 * SA_ROWS, LANES), F32),
                        pltpu.VMEM((A_GROUPS, 2 * n1 * SA_ROWS, LANES), F32)],
        compiler_params=_params("parallel", "parallel"),
        name="fft_a",
    )(x4, x4, x4, x4, f1_tab)
    nk = n1 // SB_ROWS
    z4 = misc.reshape(batch, n2, n1, N_MISC * CB)
    out = pl.pallas_call(
        functools.partial(_fft_b_kernel, scale=1.0 / math.sqrt(seq * GDIM)),
        grid=(nk, batch),
        in_specs=[pl.BlockSpec((None, A_GROUPS, SB_ROWS, n2, LANES), lambda k, b: (b, 0, k, 0, 0)),
                  pl.BlockSpec((None, A_GROUPS, SB_ROWS, n2, LANES), lambda k, b: (b, 0, nk + k, 0, 0)),
                  pl.BlockSpec((SB_ROWS, 2 * n2, 2 * n2), lambda k, b: (k, 0, 0)),
                  pl.BlockSpec((None, n2, SB_ROWS, CB), lambda k, b: (b, 0, k, MISC_AZ)),
                  pl.BlockSpec((2 * GDIM, GDIM), lambda k, b: (0, 0)),
                  pl.BlockSpec((A_GROUPS, GDIM, GDIM), lambda k, b: (0, 0, 0))],
        out_specs=pl.BlockSpec((None, n2, SB_ROWS, CB), lambda k, b: (b, 0, k, 0)),
        out_shape=jax.ShapeDtypeStruct((batch, n2, n1, CB), BF16),
        scratch_shapes=[pltpu.VMEM((A_GROUPS, n2 * SB_ROWS, LANES), F32),
                        pltpu.VMEM((A_GROUPS, 2 * GDIM, GDIM), BF16)],
        compiler_params=_params("parallel", "parallel"),
        name="fft_b",
    )(y, y, m_tab, z4, cs_tab, wmap_bf)
    return out.reshape(batch * seq, CB)


def _dft_tables(seq):
    n2 = B_CHUNK
    n1 = seq // n2
    two_pi = 2.0 * math.pi

    def cs(idx, period):
        ang = (idx % period).astype(F32) * (two_pi / period)
        return jnp.cos(ang), jnp.sin(ang)

    i1 = jnp.arange(n1, dtype=jnp.int32)
    c1, s1 = cs(i1[:, None] * i1[None, :], n1)
    f1_tab = jnp.concatenate([c1, -s1], axis=0).astype(BF16)
    k = i1[:, None, None] + n1 * jnp.arange(n2, dtype=jnp.int32)[None, :, None]
    s2 = jnp.arange(n2, dtype=jnp.int32)[None, None, :]
    cm, sm = cs(k * s2, seq)
    m_tab = jnp.concatenate([jnp.concatenate([cm, sm], axis=2),
                             jnp.concatenate([-sm, cm], axis=2)], axis=1).astype(BF16)
    ic = jnp.arange(GDIM, dtype=jnp.int32)
    cc, sc = cs(ic[:, None] * ic[None, :], GDIM)
    cs_tab = jnp.concatenate([cc, sc], axis=0).astype(BF16)
    return f1_tab, m_tab, cs_tab


Q_SUB = 128


def _attn_kernel(q_ref, kc_ref, kp_ref, kn_ref, vc_ref, vp_ref, vn_ref, o_ref, l_ref, *, seq_len, tq):
    t = pl.program_id(2)
    tk = Q_SUB + 2 * HALF_WIN
    k = jnp.concatenate([kp_ref[...], kc_ref[...], kn_ref[...]], axis=0)
    v = jnp.concatenate([vp_ref[...], vc_ref[...], vn_ref[...]], axis=0)
    ones = jnp.ones((tk, LANES), BF16)
    row = lax.broadcasted_iota(jnp.int32, (2 * Q_SUB, tk), 0) & (Q_SUB - 1)
    col = lax.broadcasted_iota(jnp.int32, (2 * Q_SUB, tk), 1)
    band = jnp.abs(row + HALF_WIN - col) <= HALF_WIN
    lo = lax.broadcasted_iota(jnp.int32, (Q_SUB, LANES), 1) < C_HEAD_DIM
    for u in range(tq // Q_SUB):
        rows = slice(u * Q_SUB, (u + 1) * Q_SUB)
        kpos = t * tq + u * Q_SUB - HALF_WIN + col
        valid = band & (kpos >= 0) & (kpos < seq_len)
        q = q_ref[rows, :]
        ku, vu = k[u * Q_SUB:u * Q_SUB + tk], v[u * Q_SUB:u * Q_SUB + tk]
        for hp in range(C_HEADS // 2):
            cols = slice(hp * LANES, (hp + 1) * LANES)
            qp = q[:, cols]
            zero = jnp.zeros_like(qp)
            q2 = jnp.concatenate([jnp.where(lo, qp, zero), jnp.where(lo, zero, qp)], axis=0)
            s = lax.dot_general(q2, ku[:, cols], (((1,), (1,)), ((), ())), preferred_element_type=F32)
            s = jnp.where(valid, s, NEG_BIG)
            m = jnp.max(s, axis=1, keepdims=True)
            p = jnp.exp2(s - m).astype(BF16)
            pv = jnp.dot(p, jnp.concatenate([vu[:, cols], ones], axis=1), preferred_element_type=F32)
            den = pv[:, LANES:]
            o = pv[:, :LANES] / den
            lse = (m + jnp.log2(den)) * math.log(2.0)
            o_ref[rows, cols] = jnp.where(lo, o[:Q_SUB], o[Q_SUB:]).astype(BF16)
            l_ref[rows, cols] = jnp.where(lo, lse[:Q_SUB], lse[Q_SUB:])


def _attn_group(qkv, g):
    batch, dil, ln, _ = qkv.shape
    tq = min(2 * Q_SUB, ln)
    per = tq // HALF_WIN
    nhalo = ln // HALF_WIN

    def cur(blk):
        return pl.BlockSpec((None, None, tq, CB), lambda b, r, t: (b, r, t, blk))

    def before(blk):
        return pl.BlockSpec((None, None, HALF_WIN, CB),
                            lambda b, r, t: (b, r, jnp.maximum(t * per - 1, 0), blk))

    def after(blk):
        return pl.BlockSpec((None, None, HALF_WIN, CB),
                            lambda b, r, t: (b, r, jnp.minimum((t + 1) * per, nhalo - 1), blk))

    tile = pl.BlockSpec((None, None, tq, CB), lambda b, r, t: (b, r, t, 0))
    return pl.pallas_call(
        functools.partial(_attn_kernel, seq_len=ln, tq=tq),
        grid=(batch, dil, ln // tq),
        in_specs=[cur(0), cur(1), before(1), after(1), cur(2), before(2), after(2)],
        out_specs=[tile, tile],
        out_shape=[jax.ShapeDtypeStruct((batch, dil, ln, CB), BF16),
                   jax.ShapeDtypeStruct((batch, dil, ln, CB), F32)],
        compiler_params=_params("parallel", "parallel", "parallel"),
        name=f"attn{g}",
    )(*([qkv] * 7))


def _combine_kernel(o0_ref, l0_ref, o1_ref, l1_ref, o2_ref, l2_ref, cz_ref, y_ref, *scr):
    tm = o0_ref.shape[0]
    for (o_ref, l_ref), (os_ref, ls_ref), dil in zip(((o1_ref, l1_ref), (o2_ref, l2_ref)),
                                                      (scr[:2], scr[2:]), C_DILATIONS[1:]):
        n = tm // dil
        for r in range(dil):
            rows = pl.ds(r, n, stride=dil)
            for c in range(NSLAB):
                cols = slice(c * LANES, (c + 1) * LANES)
                os_ref[c, rows, :] = o_ref[r, :, cols].astype(F32)
                ls_ref[c, rows, :] = l_ref[r, :, cols]
    for c in range(NSLAB):
        cols = slice(c * LANES, (c + 1) * LANES)
        l0, l1, l2 = l0_ref[:, cols], scr[1][c], scr[3][c]
        mx = jnp.maximum(jnp.maximum(l0, l1), l2)
        e0, e1, e2 = jnp.exp(l0 - mx), jnp.exp(l1 - mx), jnp.exp(l2 - mx)
        num = e0 * o0_ref[:, cols].astype(F32) + e1 * scr[0][c] + e2 * scr[2][c]
        y_ref[:, cols] = (num / (e0 + e1 + e2) * cz_ref[:, cols].astype(F32)).astype(BF16)


def _combine(res, misc, batch, seq, tm):
    (o0, l0), (o1, l1), (o2, l2) = res
    nseq = seq // tm
    nat = pl.BlockSpec((None, None, tm, CB), lambda i: (i // nseq, 0, i % nseq, 0))

    def dil_spec(dil):
        return pl.BlockSpec((None, dil, tm // dil, CB), lambda i: (i // nseq, 0, i % nseq, 0))

    s1, s2 = dil_spec(C_DILATIONS[1]), dil_spec(C_DILATIONS[2])
    return pl.pallas_call(
        _combine_kernel,
        grid=(batch * nseq,),
        in_specs=[nat, nat, s1, s1, s2, s2, pl.BlockSpec((tm, CB), lambda i: (i, MISC_CZ))],
        out_specs=pl.BlockSpec((tm, CB), lambda i: (i, 0)),
        out_shape=jax.ShapeDtypeStruct((batch * seq, CB), BF16),
        scratch_shapes=[pltpu.VMEM((NSLAB, tm, LANES), F32)] * 4,
        compiler_params=_params("parallel"),
        name="combine",
    )(o0, l0, o1, l1, o2, l2, misc)


def _merge_kernel(x_ref, ya_ref, yb_ref, yc_ref, wg0_ref, wg1_ref, wg2_ref, wb_ref, o_ref, xb_ref):
    @pl.when(pl.program_id(1) == 0)
    def _():
        xb_ref[...] = x_ref[...].astype(BF16)

    xb = xb_ref[...]
    acc = None
    for kk, (y_ref, wg_ref) in enumerate(((ya_ref, wg0_ref), (yb_ref, wg1_ref), (yc_ref, wg2_ref))):
        gate = jax.nn.sigmoid(jnp.dot(xb, wg_ref[...], preferred_element_type=F32))
        term = gate * jnp.dot(y_ref[...], wb_ref[kk], preferred_element_type=F32)
        acc = term if acc is None else acc + term
    o_ref[...] = acc.astype(BF16)


def _merge(x2, ya, yb, yc, w_gate, wbr_bf, tm):
    m = x2.shape[0]
    ncol = D_MODEL // CB
    ybs = pl.BlockSpec((tm, CB), lambda i, c: (i, 0))

    def gate_spec(kk):
        return pl.BlockSpec((D_MODEL, CB), lambda i, c: (0, kk * ncol + c))

    return pl.pallas_call(
        _merge_kernel,
        grid=(m // tm, ncol),
        in_specs=[pl.BlockSpec((tm, D_MODEL), lambda i, c: (i, 0)), ybs, ybs, ybs,
                  gate_spec(0), gate_spec(1), gate_spec(2),
                  pl.BlockSpec((N_BRANCH, W_BR, CB), lambda i, c: (0, 0, c))],
        out_specs=pl.BlockSpec((tm, CB), lambda i, c: (i, c)),
        out_shape=jax.ShapeDtypeStruct((m, D_MODEL), BF16),
        scratch_shapes=[pltpu.VMEM((tm, D_MODEL), BF16)],
        compiler_params=_params("parallel", "arbitrary"),
        name="merge",
    )(x2, ya, yb, yc, w_gate, w_gate, w_gate, wbr_bf)


def _outproj_kernel(x_ref, m_ref, wo_ref, g_ref, b_ref, o_ref):
    for rc in range(x_ref.shape[0] // ROW_CHUNK):
        rows = slice(rc * ROW_CHUNK, (rc + 1) * ROW_CHUNK)
        out = jnp.dot(m_ref[rows, :], wo_ref[...], preferred_element_type=F32)
        z = ALPHA * x_ref[rows, :] + out
        mu = jnp.mean(z, axis=-1, keepdims=True)
        zc = z - mu
        var = jnp.mean(zc * zc, axis=-1, keepdims=True)
        o_ref[rows, :] = zc * lax.rsqrt(var + LN_EPS) * g_ref[...] + b_ref[...]


def _outproj(x2, merged, wo_bf, ln_g, ln_b, tm):
    m = x2.shape[0]
    row = pl.BlockSpec((tm, D_MODEL), lambda i: (i, 0))
    vec = pl.BlockSpec((1, D_MODEL), lambda i: (0, 0))
    return pl.pallas_call(
        _outproj_kernel,
        grid=(m // tm,),
        in_specs=[row, row, pl.BlockSpec((D_MODEL, D_MODEL), lambda i: (0, 0)), vec, vec],
        out_specs=row,
        out_shape=jax.ShapeDtypeStruct((m, D_MODEL), F32),
        compiler_params=_params("parallel"),
        name="outproj",
    )(x2, merged, wo_bf, ln_g, ln_b)


def _rope_tables(seq, tm):
    half = ROPE_DIM // 2
    pos = jnp.arange(seq, dtype=F32)
    inv_freq = ROPE_THETA ** (-jnp.arange(half, dtype=F32) / half)
    ang = pos[:, None] * inv_freq[None, :]
    cos, sin = jnp.cos(ang), jnp.sin(ang)
    ones = jnp.ones((seq, C_HEAD_DIM - ROPE_DIM), F32)
    zeros = jnp.zeros((seq, C_HEAD_DIM - half), F32)
    cos_h = jnp.concatenate([cos, cos, ones], axis=1)
    lo_h = jnp.concatenate([-sin, zeros], axis=1)
    hi_h = jnp.concatenate([jnp.zeros((seq, half), F32), sin, zeros[:, :C_HEAD_DIM - ROPE_DIM]], axis=1)
    tabs = jnp.stack([jnp.tile(t, (1, LANES // C_HEAD_DIM)) for t in (cos_h, lo_h, hi_h)])

    def residue_major(dil):
        t = tabs.reshape(3, seq // tm, tm // dil, dil, LANES)
        return jnp.swapaxes(t, 2, 3).reshape(3, seq, LANES)

    return jnp.stack([residue_major(dil) for dil in C_DILATIONS])


def _mix_columns(w):
    blk = lambda off: w[:, off:off + CB]
    cols = [blk(OFF_A), blk(OFF_AZ), blk(OFF_B), blk(OFF_B + CB), blk(OFF_BZ), blk(OFF_CZ)]
    for g in range(C_NG):
        cols += [blk(OFF_C + part * C_QKV + g * CB) for part in range(3)]
    return jnp.concatenate(cols, axis=1).astype(BF16), w[:, OFF_G:].astype(BF16)


TM_INPROJ = 1024
TM_SGU = 512
TM_COMBINE = 1024
TM_MERGE = 1024
TM_OUTPROJ = 512


def _layer(x2, batch, seq, wl, rope_tab, dft_t):
    h_a, misc, g0, g1, g2 = _inproj(x2, wl["w_mix"], rope_tab, batch, seq, TM_INPROJ)
    y_a = _fourier(h_a, misc, *dft_t, wl["w_amap"], batch, seq)
    y_b = _sgu(misc, wl["sgu_ln_g"], wl["sgu_ln_b"], wl["sgu_w"], wl["sgu_bias"], TM_SGU)
    groups = (g0.reshape(batch, 1, seq, 3 * CB), g1, g2)
    y_c = _combine([_attn_group(qkv, g) for g, qkv in enumerate(groups)], misc, batch, seq, TM_COMBINE)
    merged = _merge(x2, y_a, y_b, y_c, wl["w_gate"], wl["w_branch"], TM_MERGE)
    return _outproj(x2, merged, wl["w_out"], wl["ln_g"], wl["ln_b"], TM_OUTPROJ)


def kernel(x_prompt, x_sample, w_in, w_amap, sgu_ln_g, sgu_ln_b, sgu_w, sgu_b, w_branch, w_out, ln_g, ln_b):
    layers = []
    for l in range(DEPTH):
        w_mix, w_gate = _mix_columns(w_in[l])
        layers.append({
            "w_mix": w_mix,
            "w_gate": w_gate,
            "w_amap": w_amap[l].astype(BF16),
            "sgu_ln_g": sgu_ln_g[l][None, :],
            "sgu_ln_b": sgu_ln_b[l][None, :],
            "sgu_w": sgu_w[l].astype(BF16),
            "sgu_bias": jnp.repeat(sgu_b[l].T, GDIM, axis=1),
            "w_branch": w_branch[l].astype(BF16),
            "w_out": w_out[l].astype(BF16),
            "ln_g": ln_g[l][None, :],
            "ln_b": ln_b[l][None, :],
        })
    outs = []
    for x in (x_prompt, x_sample):
        batch, seq, _ = x.shape
        rope_t = _rope_tables(seq, TM_INPROJ)
        dft_t = _dft_tables(seq)
        x2 = x.reshape(batch * seq, D_MODEL)
        for wl in layers:
            x2 = _layer(x2, batch, seq, wl, rope_t, dft_t)
        outs.append(x2.reshape(batch, seq, D_MODEL))
    return tuple(outs)
```

```python
import functools
import math

import jax
import jax.numpy as jnp
from jax import lax
from jax.experimental import pallas as pl
from jax.experimental.pallas import tpu as pltpu

F32 = jnp.float32
BF16 = jnp.bfloat16

D_MODEL = 2048
DEPTH = 2
W_BR = 512
A_GROUPS = 4
GDIM = 128
B_CHUNK = 128
C_DILATIONS = (1, 4, 16)
C_NG = len(C_DILATIONS)
C_HEADS = 8
C_HEAD_DIM = 64
C_QKV = C_NG * C_HEADS * C_HEAD_DIM
ROPE_DIM = C_HEAD_DIM // 4
ROPE_THETA = 500000.0
N_BRANCH = 3
ALPHA = (2 * DEPTH) ** 0.25
LN_EPS = 1e-5
NEG_BIG = -1e30
HALF_WIN = 64
Q_SCALE = math.log2(math.e) / math.sqrt(C_HEAD_DIM)

LANES = 128
BF16_ROWS = 16
CB = 512
NSLAB = CB // LANES
OFF_A, OFF_AZ, OFF_B, OFF_BZ, OFF_C = 0, 512, 1024, 2048, 2560
OFF_CZ = OFF_C + 3 * C_QKV
OFF_G = OFF_CZ + W_BR
N_MIX_BLOCKS = 15
MISC_AZ, MISC_U, MISC_V, MISC_BZ, MISC_CZ = 0, 1, 2, 3, 4
N_MISC = 5
J_MISC = 1
J_QKV = J_MISC + N_MISC

VMEM_LIMIT = 56 * 1024 * 1024


def _params(*sem):
    return pltpu.CompilerParams(dimension_semantics=sem, vmem_limit_bytes=VMEM_LIMIT)


def _slabs(x):
    return [x[:, c * LANES:(c + 1) * LANES] for c in range(x.shape[1] // LANES)]


def _silu(h):
    return h * jax.nn.sigmoid(h)


def _gelu_tanh(h):
    c = math.sqrt(2.0 / math.pi)
    return h * (0.5 * (1.0 + jnp.tanh(c * (h + 0.044715 * (h * h * h)))))


def _rope_slab(hc, cos, sin_lo, sin_hi):
    up = pltpu.roll(hc, LANES - ROPE_DIM // 2, axis=1)
    dn = pltpu.roll(hc, ROPE_DIM // 2, axis=1)
    return hc * cos + up * sin_lo + dn * sin_hi


ROW_CHUNK = 256
ROW_CHUNK_SMALL = 128


def _inproj_kernel(x_ref, w_ref, tab_ref, ha_ref, misc_ref, g0_ref, g1_ref, g2_ref, xb_ref, acc_ref):
    j = pl.program_id(1)
    tm = x_ref.shape[0]

    @pl.when(j == 0)
    def _():
        xb_ref[...] = x_ref[...].astype(BF16)

    def chunks(size=ROW_CHUNK_SMALL):
        for rc in range(tm // size):
            rows = slice(rc * size, (rc + 1) * size)
            yield rc, rows, jnp.dot(xb_ref[rows, :], w_ref[...], preferred_element_type=F32)

    @pl.when(j == 0)
    def _():
        for _, rows, res in chunks():
            ha_ref[rows, :] = res

    @pl.when((j == J_MISC + MISC_AZ) | (j == J_MISC + MISC_BZ) | (j == J_MISC + MISC_CZ))
    def _():
        for _, rows, res in chunks():
            misc_ref[rows, :] = _silu(res).astype(BF16)

    @pl.when((j == J_MISC + MISC_U) | (j == J_MISC + MISC_V))
    def _():
        for _, rows, res in chunks():
            misc_ref[rows, :] = _gelu_tanh(res).astype(BF16)

    for g, (dil, out_ref) in enumerate(zip(C_DILATIONS, (g0_ref, g1_ref, g2_ref))):
        for part in range(3):
            @pl.when(j == J_QKV + 3 * g + part)
            def _(g=g, dil=dil, out_ref=out_ref, rotate=part < 2, is_q=part == 0):
                size = max(ROW_CHUNK_SMALL, BF16_ROWS * dil)
                n = size // dil
                for rc, rows, res in chunks(size):
                    if dil > 1:
                        for c, slab in enumerate(_slabs(res)):
                            acc_ref[rc % 2, c, :size, :] = slab
                    for r in range(dil):
                        trows = pl.ds(r * (tm // dil) + rc * n, n)
                        tabs = [tab_ref[g, kind, trows, :] for kind in range(3)] if rotate else None
                        for c in range(NSLAB):
                            cols = slice(c * LANES, (c + 1) * LANES)
                            if dil > 1:
                                hc = acc_ref[rc % 2, c, pl.ds(r, n, stride=dil), :]
                            else:
                                hc = res[:, cols]
                            if rotate:
                                hc = _rope_slab(hc, *tabs)
                            if is_q:
                                hc = hc * Q_SCALE
                            if dil > 1:
                                out_ref[r, rc * n:(rc + 1) * n, cols] = hc.astype(BF16)
                            else:
                                out_ref[rows, cols] = hc.astype(BF16)


MIX_BLOCK_OF = tuple(off // CB for off in (
    [OFF_A, OFF_AZ, OFF_B, OFF_B + CB, OFF_BZ, OFF_CZ]
    + [OFF_C + part * C_QKV + g * CB for g in range(C_NG) for part in range(3)]))


def _mix_block(j):
    blk = jnp.int32(MIX_BLOCK_OF[0])
    for step in range(1, N_MIX_BLOCKS):
        blk = jnp.where(j == step, MIX_BLOCK_OF[step], blk)
    return blk


def _inproj(x2, w_mix, rope_tab, batch, seq, tm):
    m = x2.shape[0]
    nseq = seq // tm

    def part(j, first):
        return jnp.clip(j - first, 0, 2)

    def dil_spec(g):
        dil = C_DILATIONS[g]
        return pl.BlockSpec((None, dil, tm // dil, CB),
                            lambda i, j: (i // nseq, 0, i % nseq, part(j, J_QKV + 3 * g)))

    def dil_shape(g):
        dil = C_DILATIONS[g]
        return jax.ShapeDtypeStruct((batch, dil, seq // dil, 3 * CB), BF16)

    return pl.pallas_call(
        _inproj_kernel,
        grid=(m // tm, N_MIX_BLOCKS),
        in_specs=[
            pl.BlockSpec((tm, D_MODEL), lambda i, j: (i, 0)),
            pl.BlockSpec((D_MODEL, CB), lambda i, j: (0, _mix_block(j))),
            pl.BlockSpec((C_NG, 3, tm, LANES), lambda i, j: (0, 0, i % nseq, 0)),
        ],
        out_specs=[
            pl.BlockSpec((tm, CB), lambda i, j: (i, 0)),
            pl.BlockSpec((tm, CB), lambda i, j: (i, jnp.clip(j - J_MISC, 0, N_MISC - 1))),
            pl.BlockSpec((tm, CB), lambda i, j: (i, part(j, J_QKV))),
            dil_spec(1), dil_spec(2),
        ],
        out_shape=[
            jax.ShapeDtypeStruct((m, CB), F32),
            jax.ShapeDtypeStruct((m, N_MISC * CB), BF16),
            jax.ShapeDtypeStruct((m, 3 * CB), BF16),
            dil_shape(1), dil_shape(2),
        ],
        scratch_shapes=[pltpu.VMEM((tm, D_MODEL), BF16), pltpu.VMEM((2, NSLAB, ROW_CHUNK, LANES), F32)],
        compiler_params=_params("parallel", "arbitrary"),
        name="inproj",
    )(x2, w_mix, rope_tab)


def _sgu_kernel(u_ref, v_ref, z_ref, g_ref, b_ref, ws_ref, bs_ref, o_ref):
    tm = u_ref.shape[0]
    v = v_ref[...].astype(F32)
    mu = jnp.mean(v, axis=-1, keepdims=True)
    vc = v - mu
    var = jnp.mean(vc * vc, axis=-1, keepdims=True)
    vn = (vc * lax.rsqrt(var + LN_EPS) * g_ref[...] + b_ref[...]).astype(BF16)
    bias = bs_ref[...]
    for n in range(tm // B_CHUNK):
        rows = slice(n * B_CHUNK, (n + 1) * B_CHUNK)
        mixed = jnp.concatenate(
            [jnp.dot(ws_ref[g], vn[rows, g * GDIM:(g + 1) * GDIM], preferred_element_type=F32)
             for g in range(A_GROUPS)], axis=1) + bias
        o_ref[rows, :] = (u_ref[rows, :].astype(F32) * mixed * z_ref[rows, :].astype(F32)).astype(BF16)


def _sgu(misc, ln_g, ln_b, ws_bf, bias_full, tm):
    m = misc.shape[0]
    col = lambda blk: pl.BlockSpec((tm, CB), lambda i: (i, blk))
    full = lambda shape: pl.BlockSpec(shape, lambda i: (0,) * len(shape))
    return pl.pallas_call(
        _sgu_kernel,
        grid=(m // tm,),
        in_specs=[col(MISC_U), col(MISC_V), col(MISC_BZ), full((1, CB)), full((1, CB)),
                  full((A_GROUPS, B_CHUNK, B_CHUNK)), full((B_CHUNK, CB))],
        out_specs=pl.BlockSpec((tm, CB), lambda i: (i, 0)),
        out_shape=jax.ShapeDtypeStruct((m, CB), BF16),
        compiler_params=_params("parallel"),
        name="sgu",
    )(misc, misc, misc, ln_g, ln_b, ws_bf, bias_full)


SA_ROWS = 8
SB_ROWS = BF16_ROWS


def _fft_a_kernel(x0_ref, x1_ref, x2_ref, x3_ref, f_ref, y_ref, xin_ref, yout_ref):
    n1 = x0_ref.shape[0]
    for g, x_ref in enumerate((x0_ref, x1_ref, x2_ref, x3_ref)):
        xin_ref[g] = x_ref[...].reshape(n1 * SA_ROWS, LANES)
    for jj in range(SA_ROWS):
        xj = jnp.concatenate([xin_ref[g, pl.ds(jj, n1, stride=SA_ROWS), :] for g in range(A_GROUPS)],
                             axis=1).astype(BF16)
        yj = jnp.dot(f_ref[...], xj, preferred_element_type=F32)
        for g, slab in enumerate(_slabs(yj)):
            yout_ref[g, pl.ds(jj, 2 * n1, stride=SA_ROWS), :] = slab
    for g in range(A_GROUPS):
        y_ref[g] = yout_ref[g].reshape(2 * n1, SA_ROWS, LANES)


def _fft_b_kernel(yr_ref, yi_ref, m_ref, z_ref, cs_ref, wm_ref, o_ref, f_ref, csw_ref, *, scale):
    n2 = yr_ref.shape[2]
    for g in range(A_GROUPS):
        csw_ref[g] = (jnp.dot(cs_ref[...], wm_ref[g], preferred_element_type=F32) * scale).astype(BF16)

    def body(kk, carry):
        ycat = jnp.concatenate(
            [jnp.concatenate([y_ref[g, kk] for g in range(A_GROUPS)], axis=1) for y_ref in (yr_ref, yi_ref)],
            axis=0).astype(BF16)
        zz = jnp.dot(m_ref[kk], ycat, preferred_element_type=F32).astype(BF16)
        for g in range(A_GROUPS):
            cols = slice(g * GDIM, (g + 1) * GDIM)
            z_ri = jnp.concatenate([zz[:n2, cols], zz[n2:, cols]], axis=1)
            f_ref[g, pl.ds(kk, n2, stride=SB_ROWS), :] = jnp.dot(
                z_ri, csw_ref[g], preferred_element_type=F32)
        return carry

    lax.fori_loop(0, SB_ROWS, body, 0)
    mixed = jnp.concatenate([f_ref[g] for g in range(A_GROUPS)], axis=1)
    gate = z_ref[...].reshape(n2 * SB_ROWS, CB).astype(F32)
    o_ref[...] = (mixed * gate).astype(BF16).reshape(n2, SB_ROWS, CB)


def _fourier(h_a, misc, f1_tab, m_tab, cs_tab, wmap_bf, batch, seq):
    n2 = B_CHUNK
    n1 = seq // n2
    x4 = h_a.reshape(batch, n1, n2, CB)
    xspec = lambda g: pl.BlockSpec((None, n1, SA_ROWS, LANES), lambda b, t: (b, 0, t, g))
    y = pl.pallas_call(
        _fft_a_kernel,
        grid=(batch, n2 // SA_ROWS),
        in_specs=[xspec(0), xspec(1), xspec(2), xspec(3),
                  pl.BlockSpec((2 * n1, n1), lambda b, t: (0, 0))],
        out_specs=pl.BlockSpec((None, A_GROUPS, 2 * n1, SA_ROWS, LANES), lambda b, t: (b, 0, 0, t, 0)),
        out_shape=jax.ShapeDtypeStruct((batch, A_GROUPS, 2 * n1, n2, LANES), F32),
        scratch_shapes=[pltpu.VMEM((A_GROUPS, n1 * SA_ROWS, LANES), F32),
                        pltpu.VMEM((A_GROUPS, 2 * n1 * SA_ROWS, LANES), F32)],
        compiler_params=_params("parallel", "parallel"),
        name="fft_a",
    )(x4, x4, x4, x4, f1_tab)
    nk = n1 // SB_ROWS
    z4 = misc.reshape(batch, n2, n1, N_MISC * CB)
    out = pl.pallas_call(
        functools.partial(_fft_b_kernel, scale=1.0 / math.sqrt(seq * GDIM)),
        grid=(nk, batch),
        in_specs=[pl.BlockSpec((None, A_GROUPS, SB_ROWS, n2, LANES), lambda k, b: (b, 0, k, 0, 0)),
                  pl.BlockSpec((None, A_GROUPS, SB_ROWS, n2, LANES), lambda k, b: (b, 0, nk + k, 0, 0)),
                  pl.BlockSpec((SB_ROWS, 2 * n2, 2 * n2), lambda k, b: (k, 0, 0)),
                  pl.BlockSpec((None, n2, SB_ROWS, CB), lambda k, b: (b, 0, k, MISC_AZ)),
                  pl.BlockSpec((2 * GDIM, GDIM), lambda k, b: (0, 0)),
                  pl.BlockSpec((A_GROUPS, GDIM, GDIM), lambda k, b: (0, 0, 0))],
        out_specs=pl.BlockSpec((None, n2, SB_ROWS, CB), lambda k, b: (b, 0, k, 0)),
        out_shape=jax.ShapeDtypeStruct((batch, n2, n1, CB), BF16),
        scratch_shapes=[pltpu.VMEM((A_GROUPS, n2 * SB_ROWS, LANES), F32),
                        pltpu.VMEM((A_GROUPS, 2 * GDIM, GDIM), BF16)],
        compiler_params=_params("parallel", "parallel"),
        name="fft_b",
    )(y, y, m_tab, z4, cs_tab, wmap_bf)
    return out.reshape(batch * seq, CB)


def _dft_tables(seq):
    n2 = B_CHUNK
    n1 = seq // n2
    two_pi = 2.0 * math.pi

    def cs(idx, period):
        ang = (idx % period).astype(F32) * (two_pi / period)
        return jnp.cos(ang), jnp.sin(ang)

    i1 = jnp.arange(n1, dtype=jnp.int32)
    c1, s1 = cs(i1[:, None] * i1[None, :], n1)
    f1_tab = jnp.concatenate([c1, -s1], axis=0).astype(BF16)
    k = i1[:, None, None] + n1 * jnp.arange(n2, dtype=jnp.int32)[None, :, None]
    s2 = jnp.arange(n2, dtype=jnp.int32)[None, None, :]
    cm, sm = cs(k * s2, seq)
    m_tab = jnp.concatenate([jnp.concatenate([cm, sm], axis=2),
                             jnp.concatenate([-sm, cm], axis=2)], axis=1).astype(BF16)
    ic = jnp.arange(GDIM, dtype=jnp.int32)
    cc, sc = cs(ic[:, None] * ic[None, :], GDIM)
    cs_tab = jnp.concatenate([cc, sc], axis=0).astype(BF16)
    return f1_tab, m_tab, cs_tab


Q_SUB = 128
ATTN_ROWS = 512


def _attn_kernel(q_ref, kc_ref, kp_ref, kn_ref, vc_ref, vp_ref, vn_ref, o_ref, l_ref, *, seq_len, tq):
    t = pl.program_id(2)
    tk = Q_SUB + 2 * HALF_WIN
    k = jnp.concatenate([kp_ref[...], kc_ref[...], kn_ref[...]], axis=0)
    v = jnp.concatenate([vp_ref[...], vc_ref[...], vn_ref[...]], axis=0)
    ones = jnp.ones((tk, LANES), BF16)
    row = lax.broadcasted_iota(jnp.int32, (2 * Q_SUB, tk), 0) & (Q_SUB - 1)
    col = lax.broadcasted_iota(jnp.int32, (2 * Q_SUB, tk), 1)
    band = jnp.abs(row + HALF_WIN - col) <= HALF_WIN
    lo = lax.broadcasted_iota(jnp.int32, (Q_SUB, LANES), 1) < C_HEAD_DIM
    for u in range(tq // Q_SUB):
        rows = slice(u * Q_SUB, (u + 1) * Q_SUB)
        kpos = t * tq + u * Q_SUB - HALF_WIN + col[:1]
        valid = band & ((kpos >= 0) & (kpos < seq_len))
        q = q_ref[rows, :]
        ku, vu = k[u * Q_SUB:u * Q_SUB + tk], v[u * Q_SUB:u * Q_SUB + tk]
        for hp in range(C_HEADS // 2):
            cols = slice(hp * LANES, (hp + 1) * LANES)
            qp = q[:, cols]
            zero = jnp.zeros_like(qp)
            q2 = jnp.concatenate([jnp.where(lo, qp, zero), jnp.where(lo, zero, qp)], axis=0)
            s = lax.dot_general(q2, ku[:, cols], (((1,), (1,)), ((), ())), preferred_element_type=F32)
            s = jnp.where(valid, s, NEG_BIG)
            m = jnp.max(s, axis=1, keepdims=True)
            p = jnp.exp2(s - m).astype(BF16)
            pv = jnp.dot(p, jnp.concatenate([vu[:, cols], ones], axis=1), preferred_element_type=F32)
            den = pv[:, LANES:]
            o = pv[:, :LANES] / den
            lse = (m + jnp.log2(den)) * math.log(2.0)
            o_ref[rows, cols] = jnp.where(lo, o[:Q_SUB], o[Q_SUB:]).astype(BF16)
            l_ref[rows, cols] = jnp.where(lo, lse[:Q_SUB], lse[Q_SUB:])


def _attn_group(qkv, g):
    batch, dil, ln, _ = qkv.shape
    tq = min(ATTN_ROWS, ln)
    per = tq // HALF_WIN
    nhalo = ln // HALF_WIN

    def cur(blk):
        return pl.BlockSpec((None, None, tq, CB), lambda b, r, t: (b, r, t, blk))

    def before(blk):
        return pl.BlockSpec((None, None, HALF_WIN, CB),
                            lambda b, r, t: (b, r, jnp.maximum(t * per - 1, 0), blk))

    def after(blk):
        return pl.BlockSpec((None, None, HALF_WIN, CB),
                            lambda b, r, t: (b, r, jnp.minimum((t + 1) * per, nhalo - 1), blk))

    tile = pl.BlockSpec((None, None, tq, CB), lambda b, r, t: (b, r, t, 0))
    return pl.pallas_call(
        functools.partial(_attn_kernel, seq_len=ln, tq=tq),
        grid=(batch, dil, ln // tq),
        in_specs=[cur(0), cur(1), before(1), after(1), cur(2), before(2), after(2)],
        out_specs=[tile, tile],
        out_shape=[jax.ShapeDtypeStruct((batch, dil, ln, CB), BF16),
                   jax.ShapeDtypeStruct((batch, dil, ln, CB), F32)],
        compiler_params=_params("parallel", "parallel", "parallel"),
        name=f"attn{g}",
    )(*([qkv] * 7))


def _combine_kernel(o0_ref, l0_ref, o1_ref, l1_ref, o2_ref, l2_ref, cz_ref, y_ref, *scr):
    tm = o0_ref.shape[0]
    for (o_ref, l_ref), (os_ref, ls_ref), dil in zip(((o1_ref, l1_ref), (o2_ref, l2_ref)),
                                                      (scr[:2], scr[2:]), C_DILATIONS[1:]):
        n = tm // dil
        for r in range(dil):
            rows = pl.ds(r, n, stride=dil)
            for c in range(NSLAB):
                cols = slice(c * LANES, (c + 1) * LANES)
                os_ref[c, rows, :] = o_ref[r, :, cols].astype(F32)
                ls_ref[c, rows, :] = l_ref[r, :, cols]
    for c in range(NSLAB):
        cols = slice(c * LANES, (c + 1) * LANES)
        l0, l1, l2 = l0_ref[:, cols], scr[1][c], scr[3][c]
        mx = jnp.maximum(jnp.maximum(l0, l1), l2)
        e0, e1, e2 = jnp.exp(l0 - mx), jnp.exp(l1 - mx), jnp.exp(l2 - mx)
        num = e0 * o0_ref[:, cols].astype(F32) + e1 * scr[0][c] + e2 * scr[2][c]
        y_ref[:, cols] = (num / (e0 + e1 + e2) * cz_ref[:, cols].astype(F32)).astype(BF16)


def _combine(res, misc, batch, seq, tm):
    (o0, l0), (o1, l1), (o2, l2) = res
    nseq = seq // tm
    nat = pl.BlockSpec((None, None, tm, CB), lambda i: (i // nseq, 0, i % nseq, 0))

    def dil_spec(dil):
        return pl.BlockSpec((None, dil, tm // dil, CB), lambda i: (i // nseq, 0, i % nseq, 0))

    s1, s2 = dil_spec(C_DILATIONS[1]), dil_spec(C_DILATIONS[2])
    return pl.pallas_call(
        _combine_kernel,
        grid=(batch * nseq,),
        in_specs=[nat, nat, s1, s1, s2, s2, pl.BlockSpec((tm, CB), lambda i: (i, MISC_CZ))],
        out_specs=pl.BlockSpec((tm, CB), lambda i: (i, 0)),
        out_shape=jax.ShapeDtypeStruct((batch * seq, CB), BF16),
        scratch_shapes=[pltpu.VMEM((NSLAB, tm, LANES), F32)] * 4,
        compiler_params=_params("parallel"),
        name="combine",
    )(o0, l0, o1, l1, o2, l2, misc)


def _merge_kernel(x_ref, ya_ref, yb_ref, yc_ref, wg0_ref, wg1_ref, wg2_ref, wb_ref, o_ref, xb_ref):
    @pl.when(pl.program_id(1) == 0)
    def _():
        xb_ref[...] = x_ref[...].astype(BF16)

    xb = xb_ref[...]
    acc = None
    for kk, (y_ref, wg_ref) in enumerate(((ya_ref, wg0_ref), (yb_ref, wg1_ref), (yc_ref, wg2_ref))):
        gate = jax.nn.sigmoid(jnp.dot(xb, wg_ref[...], preferred_element_type=F32))
        term = gate * jnp.dot(y_ref[...], wb_ref[kk], preferred_element_type=F32)
        acc = term if acc is None else acc + term
    o_ref[...] = acc.astype(BF16)


def _merge(x2, ya, yb, yc, w_gate, wbr_bf, tm):
    m = x2.shape[0]
    ncol = D_MODEL // CB
    ybs = pl.BlockSpec((tm, CB), lambda i, c: (i, 0))

    def gate_spec(kk):
        return pl.BlockSpec((D_MODEL, CB), lambda i, c: (0, OFF_G // CB + kk * ncol + c))

    return pl.pallas_call(
        _merge_kernel,
        grid=(m // tm, ncol),
        in_specs=[pl.BlockSpec((tm, D_MODEL), lambda i, c: (i, 0)), ybs, ybs, ybs,
                  gate_spec(0), gate_spec(1), gate_spec(2),
                  pl.BlockSpec((N_BRANCH, W_BR, CB), lambda i, c: (0, 0, c))],
        out_specs=pl.BlockSpec((tm, CB), lambda i, c: (i, c)),
        out_shape=jax.ShapeDtypeStruct((m, D_MODEL), BF16),
        scratch_shapes=[pltpu.VMEM((tm, D_MODEL), BF16)],
        compiler_params=_params("parallel", "arbitrary"),
        name="merge",
    )(x2, ya, yb, yc, w_gate, w_gate, w_gate, wbr_bf)


def _outproj_kernel(x_ref, m_ref, wo_ref, g_ref, b_ref, o_ref):
    for rc in range(x_ref.shape[0] // ROW_CHUNK):
        rows = slice(rc * ROW_CHUNK, (rc + 1) * ROW_CHUNK)
        out = jnp.dot(m_ref[rows, :], wo_ref[...], preferred_element_type=F32)
        z = ALPHA * x_ref[rows, :] + out
        mu = jnp.mean(z, axis=-1, keepdims=True)
        zc = z - mu
        var = jnp.mean(zc * zc, axis=-1, keepdims=True)
        o_ref[rows, :] = zc * lax.rsqrt(var + LN_EPS) * g_ref[...] + b_ref[...]


def _outproj(x2, merged, wo_bf, ln_g, ln_b, tm):
    m = x2.shape[0]
    row = pl.BlockSpec((tm, D_MODEL), lambda i: (i, 0))
    vec = pl.BlockSpec((1, D_MODEL), lambda i: (0, 0))
    return pl.pallas_call(
        _outproj_kernel,
        grid=(m // tm,),
        in_specs=[row, row,
                  pl.BlockSpec((D_MODEL, D_MODEL), lambda i: (0, 0), pipeline_mode=pl.Buffered(1)),
                  vec, vec],
        out_specs=row,
        out_shape=jax.ShapeDtypeStruct((m, D_MODEL), F32),
        compiler_params=_params("parallel"),
        name="outproj",
    )(x2, merged, wo_bf, ln_g, ln_b)


def _rope_tables(seq, tm):
    half = ROPE_DIM // 2
    pos = jnp.arange(seq, dtype=F32)
    inv_freq = ROPE_THETA ** (-jnp.arange(half, dtype=F32) / half)
    ang = pos[:, None] * inv_freq[None, :]
    cos, sin = jnp.cos(ang), jnp.sin(ang)
    ones = jnp.ones((seq, C_HEAD_DIM - ROPE_DIM), F32)
    zeros = jnp.zeros((seq, C_HEAD_DIM - half), F32)
    cos_h = jnp.concatenate([cos, cos, ones], axis=1)
    lo_h = jnp.concatenate([-sin, zeros], axis=1)
    hi_h = jnp.concatenate([jnp.zeros((seq, half), F32), sin, zeros[:, :C_HEAD_DIM - ROPE_DIM]], axis=1)
    tabs = jnp.stack([jnp.tile(t, (1, LANES // C_HEAD_DIM)) for t in (cos_h, lo_h, hi_h)])

    def residue_major(dil):
        t = tabs.reshape(3, seq // tm, tm // dil, dil, LANES)
        return jnp.swapaxes(t, 2, 3).reshape(3, seq, LANES)

    return jnp.stack([residue_major(dil) for dil in C_DILATIONS])


TM_INPROJ = 1024
TM_SGU = 512
TM_COMBINE = 1024
TM_MERGE = 1024
TM_OUTPROJ = 1024


def _layer(x2, batch, seq, wl, rope_tab, dft_t):
    h_a, misc, g0, g1, g2 = _inproj(x2, wl["w_mix"], rope_tab, batch, seq, TM_INPROJ)
    y_a = _fourier(h_a, misc, *dft_t, wl["w_amap"], batch, seq)
    y_b = _sgu(misc, wl["sgu_ln_g"], wl["sgu_ln_b"], wl["sgu_w"], wl["sgu_bias"], TM_SGU)
    groups = (g0.reshape(batch, 1, seq, 3 * CB), g1, g2)
    y_c = _combine([_attn_group(qkv, g) for g, qkv in enumerate(groups)], misc, batch, seq, TM_COMBINE)
    merged = _merge(x2, y_a, y_b, y_c, wl["w_gate"], wl["w_branch"], TM_MERGE)
    return _outproj(x2, merged, wl["w_out"], wl["ln_g"], wl["ln_b"], TM_OUTPROJ)


def kernel(x_prompt, x_sample, w_in, w_amap, sgu_ln_g, sgu_ln_b, sgu_w, sgu_b, w_branch, w_out, ln_g, ln_b):
    layers = []
    for l in range(DEPTH):
        w_bf = w_in[l].astype(BF16)
        layers.append({
            "w_mix": w_bf,
            "w_gate": w_bf,
            "w_amap": w_amap[l].astype(BF16),
            "sgu_ln_g": sgu_ln_g[l][None, :],
            "sgu_ln_b": sgu_ln_b[l][None, :],
            "sgu_w": sgu_w[l].astype(BF16),
            "sgu_bias": jnp.repeat(sgu_b[l].T, GDIM, axis=1),
            "w_branch": w_branch[l].astype(BF16),
            "w_out": w_out[l].astype(BF16),
            "ln_g": ln_g[l][None, :],
            "ln_b": ln_b[l][None, :],
        })
    outs = []
    for x in (x_prompt, x_sample):
        batch, seq, _ = x.shape
        rope_t = _rope_tables(seq, TM_INPROJ)
        dft_t = _dft_tables(seq)
        x2 = x.reshape(batch * seq, D_MODEL)
        for wl in layers:
            x2 = _layer(x2, batch, seq, wl, rope_t, dft_t)
        outs.append(x2.reshape(batch, seq, D_MODEL))
    return tuple(outs)
```

```python
import functools
import math

import jax
import jax.numpy as jnp
from jax import lax
from jax.experimental import pallas as pl
from jax.experimental.pallas import tpu as pltpu

F32 = jnp.float32
BF16 = jnp.bfloat16

D_MODEL = 2048
DEPTH = 2
W_BR = 512
A_GROUPS = 4
GDIM = 128
B_CHUNK = 128
C_DILATIONS = (1, 4, 16)
C_NG = len(C_DILATIONS)
C_HEADS = 8
C_HEAD_DIM = 64
C_QKV = C_NG * C_HEADS * C_HEAD_DIM
ROPE_DIM = C_HEAD_DIM // 4
ROPE_THETA = 500000.0
N_BRANCH = 3
ALPHA = (2 * DEPTH) ** 0.25
LN_EPS = 1e-5
NEG_BIG = -1e30
HALF_WIN = 64
Q_SCALE = math.log2(math.e) / math.sqrt(C_HEAD_DIM)

LANES = 128
BF16_ROWS = 16
CB = 512
NSLAB = CB // LANES
OFF_A, OFF_AZ, OFF_B, OFF_BZ, OFF_C = 0, 512, 1024, 2048, 2560
OFF_CZ = OFF_C + 3 * C_QKV
OFF_G = OFF_CZ + W_BR
N_MIX_BLOCKS = 15
MISC_AZ, MISC_U, MISC_V, MISC_BZ, MISC_CZ = 0, 1, 2, 3, 4
N_MISC = 5
J_MISC = 1
J_QKV = J_MISC + N_MISC

VMEM_LIMIT = 56 * 1024 * 1024


def _params(*sem):
    return pltpu.CompilerParams(dimension_semantics=sem, vmem_limit_bytes=VMEM_LIMIT)


def _slabs(x):
    return [x[:, c * LANES:(c + 1) * LANES] for c in range(x.shape[1] // LANES)]


def _silu(h):
    return h * jax.nn.sigmoid(h)


def _gelu_tanh(h):
    c = math.sqrt(2.0 / math.pi)
    return h * (0.5 * (1.0 + jnp.tanh(c * (h + 0.044715 * (h * h * h)))))


def _rope_slab(hc, cos, sin_lo, sin_hi):
    up = pltpu.roll(hc, LANES - ROPE_DIM // 2, axis=1)
    dn = pltpu.roll(hc, ROPE_DIM // 2, axis=1)
    return hc * cos + up * sin_lo + dn * sin_hi


ROW_CHUNK = 256


def _inproj_kernel(x_ref, w_ref, tab_ref, ha_ref, misc_ref, g0_ref, g1_ref, g2_ref, xb_ref, acc_ref):
    j = pl.program_id(1)
    tm = x_ref.shape[0]

    @pl.when(j == 0)
    def _():
        xb_ref[...] = x_ref[...].astype(BF16)

    def chunks(size=ROW_CHUNK):
        for rc in range(tm // size):
            rows = slice(rc * size, (rc + 1) * size)
            yield rc, rows, jnp.dot(xb_ref[rows, :], w_ref[...], preferred_element_type=F32)

    @pl.when(j == 0)
    def _():
        for _, rows, res in chunks():
            ha_ref[rows, :] = res

    @pl.when((j == J_MISC + MISC_AZ) | (j == J_MISC + MISC_BZ) | (j == J_MISC + MISC_CZ))
    def _():
        for _, rows, res in chunks():
            misc_ref[rows, :] = _silu(res).astype(BF16)

    @pl.when((j == J_MISC + MISC_U) | (j == J_MISC + MISC_V))
    def _():
        for _, rows, res in chunks():
            misc_ref[rows, :] = _gelu_tanh(res).astype(BF16)

    for g, (dil, out_ref) in enumerate(zip(C_DILATIONS, (g0_ref, g1_ref, g2_ref))):
        for part in range(3):
            @pl.when(j == J_QKV + 3 * g + part)
            def _(g=g, dil=dil, out_ref=out_ref, rotate=part < 2, is_q=part == 0):
                size = max(ROW_CHUNK, BF16_ROWS * dil)
                n = size // dil
                for rc, rows, res in chunks(size):
                    if dil > 1:
                        for c, slab in enumerate(_slabs(res)):
                            acc_ref[rc % 2, c, :size, :] = slab
                    for r in range(dil):
                        trows = pl.ds(r * (tm // dil) + rc * n, n)
                        tabs = [tab_ref[g, kind, trows, :] for kind in range(3)] if rotate else None
                        for c in range(NSLAB):
                            cols = slice(c * LANES, (c + 1) * LANES)
                            if dil > 1:
                                hc = acc_ref[rc % 2, c, pl.ds(r, n, stride=dil), :]
                            else:
                                hc = res[:, cols]
                            if rotate:
                                hc = _rope_slab(hc, *tabs)
                            if is_q:
                                hc = hc * Q_SCALE
                            if dil > 1:
                                out_ref[r, rc * n:(rc + 1) * n, cols] = hc.astype(BF16)
                            else:
                                out_ref[rows, cols] = hc.astype(BF16)


MIX_BLOCK_OF = tuple(off // CB for off in (
    [OFF_A, OFF_AZ, OFF_B, OFF_B + CB, OFF_BZ, OFF_CZ]
    + [OFF_C + part * C_QKV + g * CB for g in range(C_NG) for part in range(3)]))


def _mix_block(j):
    blk = jnp.int32(MIX_BLOCK_OF[0])
    for step in range(1, N_MIX_BLOCKS):
        blk = jnp.where(j == step, MIX_BLOCK_OF[step], blk)
    return blk


def _inproj(x2, w_all, layer, rope_tab, batch, seq, tm):
    m = x2.shape[0]
    nseq = seq // tm

    def part(j, first):
        return jnp.clip(j - first, 0, 2)

    def dil_spec(g):
        dil = C_DILATIONS[g]
        return pl.BlockSpec((None, dil, tm // dil, CB),
                            lambda i, j: (i // nseq, 0, i % nseq, part(j, J_QKV + 3 * g)))

    def dil_shape(g):
        dil = C_DILATIONS[g]
        return jax.ShapeDtypeStruct((batch, dil, seq // dil, 3 * CB), BF16)

    return pl.pallas_call(
        _inproj_kernel,
        grid=(m // tm, N_MIX_BLOCKS),
        in_specs=[
            pl.BlockSpec((tm, D_MODEL), lambda i, j: (i, 0)),
            pl.BlockSpec((None, D_MODEL, CB), lambda i, j: (layer, 0, _mix_block(j))),
            pl.BlockSpec((C_NG, 3, tm, LANES), lambda i, j: (0, 0, i % nseq, 0)),
        ],
        out_specs=[
            pl.BlockSpec((tm, CB), lambda i, j: (i, 0)),
            pl.BlockSpec((tm, CB), lambda i, j: (i, jnp.clip(j - J_MISC, 0, N_MISC - 1))),
            pl.BlockSpec((tm, CB), lambda i, j: (i, part(j, J_QKV))),
            dil_spec(1), dil_spec(2),
        ],
        out_shape=[
            jax.ShapeDtypeStruct((m, CB), F32),
            jax.ShapeDtypeStruct((m, N_MISC * CB), BF16),
            jax.ShapeDtypeStruct((m, 3 * CB), BF16),
            dil_shape(1), dil_shape(2),
        ],
        scratch_shapes=[pltpu.VMEM((tm, D_MODEL), BF16), pltpu.VMEM((2, NSLAB, ROW_CHUNK, LANES), F32)],
        compiler_params=_params("parallel", "arbitrary"),
        name="inproj",
    )(x2, w_all, rope_tab)


def _sgu_kernel(u_ref, v_ref, z_ref, g_ref, b_ref, ws_ref, bs_ref, o_ref):
    tm = u_ref.shape[0]
    v = v_ref[...].astype(F32)
    mu = jnp.mean(v, axis=-1, keepdims=True)
    vc = v - mu
    var = jnp.mean(vc * vc, axis=-1, keepdims=True)
    vn = (vc * lax.rsqrt(var + LN_EPS) * g_ref[...] + b_ref[...]).astype(BF16)
    bias = bs_ref[...]
    nchunk = tm // B_CHUNK
    mixed = []
    for g in range(A_GROUPS):
        cols = slice(g * GDIM, (g + 1) * GDIM)
        side = jnp.concatenate([vn[n * B_CHUNK:(n + 1) * B_CHUNK, cols] for n in range(nchunk)], axis=1)
        mixed.append(jnp.dot(ws_ref[g], side, preferred_element_type=F32))
    for n in range(nchunk):
        rows = slice(n * B_CHUNK, (n + 1) * B_CHUNK)
        mix = jnp.concatenate([mixed[g][:, n * GDIM:(n + 1) * GDIM] for g in range(A_GROUPS)], axis=1) + bias
        o_ref[rows, :] = (u_ref[rows, :].astype(F32) * mix * z_ref[rows, :].astype(F32)).astype(BF16)


def _sgu(misc, ln_g, ln_b, ws_bf, bias_full, tm):
    m = misc.shape[0]
    col = lambda blk: pl.BlockSpec((tm, CB), lambda i: (i, blk))
    full = lambda shape: pl.BlockSpec(shape, lambda i: (0,) * len(shape))
    return pl.pallas_call(
        _sgu_kernel,
        grid=(m // tm,),
        in_specs=[col(MISC_U), col(MISC_V), col(MISC_BZ), full((1, CB)), full((1, CB)),
                  full((A_GROUPS, B_CHUNK, B_CHUNK)), full((B_CHUNK, CB))],
        out_specs=pl.BlockSpec((tm, CB), lambda i: (i, 0)),
        out_shape=jax.ShapeDtypeStruct((m, CB), BF16),
        compiler_params=_params("parallel"),
        name="sgu",
    )(misc, misc, misc, ln_g, ln_b, ws_bf, bias_full)


SA_ROWS = 8
SA_BLOCKS = 4
SB_ROWS = BF16_ROWS


def _fft_a_kernel(x0_ref, x1_ref, x2_ref, x3_ref, f_ref, y_ref, xin_ref, yout_ref):
    n1 = x0_ref.shape[0]
    for blk in range(SA_BLOCKS):
        srows = slice(blk * SA_ROWS, (blk + 1) * SA_ROWS)
        for g, x_ref in enumerate((x0_ref, x1_ref, x2_ref, x3_ref)):
            xin_ref[blk, g] = x_ref[:, srows, :].reshape(n1 * SA_ROWS, LANES)
        for jj in range(SA_ROWS):
            xj = jnp.concatenate(
                [xin_ref[blk, g, pl.ds(jj, n1, stride=SA_ROWS), :] for g in range(A_GROUPS)],
                axis=1).astype(BF16)
            yj = jnp.dot(f_ref[...], xj, preferred_element_type=F32)
            for g, slab in enumerate(_slabs(yj)):
                yout_ref[blk, g, pl.ds(jj, 2 * n1, stride=SA_ROWS), :] = slab
        for g in range(A_GROUPS):
            y_ref[g, :, srows, :] = yout_ref[blk, g].reshape(2 * n1, SA_ROWS, LANES)


def _fft_b_kernel(yr_ref, yi_ref, m_ref, z_ref, cs_ref, wm_ref, o_ref, f_ref, csw_ref, *, scale):
    n2 = yr_ref.shape[2]
    for g in range(A_GROUPS):
        csw_ref[g] = (jnp.dot(cs_ref[...], wm_ref[g], preferred_element_type=F32) * scale).astype(BF16)

    def body(kk, carry):
        ycat = jnp.concatenate(
            [jnp.concatenate([y_ref[g, kk] for g in range(A_GROUPS)], axis=1) for y_ref in (yr_ref, yi_ref)],
            axis=0).astype(BF16)
        zz = jnp.dot(m_ref[kk], ycat, preferred_element_type=F32).astype(BF16)
        for g in range(A_GROUPS):
            cols = slice(g * GDIM, (g + 1) * GDIM)
            z_ri = jnp.concatenate([zz[:n2, cols], zz[n2:, cols]], axis=1)
            f_ref[g, pl.ds(kk, n2, stride=SB_ROWS), :] = jnp.dot(
                z_ri, csw_ref[g], preferred_element_type=F32)
        return carry

    lax.fori_loop(0, SB_ROWS, body, 0)
    mixed = jnp.concatenate([f_ref[g] for g in range(A_GROUPS)], axis=1)
    gate = z_ref[...].reshape(n2 * SB_ROWS, CB).astype(F32)
    o_ref[...] = (mixed * gate).astype(BF16).reshape(n2, SB_ROWS, CB)


def _fourier(h_a, misc, f1_tab, m_tab, cs_tab, wmap_bf, batch, seq):
    n2 = B_CHUNK
    n1 = seq // n2
    x4 = h_a.reshape(batch, n1, n2, CB)
    step_rows = SA_ROWS * SA_BLOCKS
    xspec = lambda g: pl.BlockSpec((None, n1, step_rows, LANES), lambda b, t: (b, 0, t, g))
    y = pl.pallas_call(
        _fft_a_kernel,
        grid=(batch, n2 // step_rows),
        in_specs=[xspec(0), xspec(1), xspec(2), xspec(3),
                  pl.BlockSpec((2 * n1, n1), lambda b, t: (0, 0))],
        out_specs=pl.BlockSpec((None, A_GROUPS, 2 * n1, step_rows, LANES), lambda b, t: (b, 0, 0, t, 0)),
        out_shape=jax.ShapeDtypeStruct((batch, A_GROUPS, 2 * n1, n2, LANES), F32),
        scratch_shapes=[pltpu.VMEM((SA_BLOCKS, A_GROUPS, n1 * SA_ROWS, LANES), F32),
                        pltpu.VMEM((SA_BLOCKS, A_GROUPS, 2 * n1 * SA_ROWS, LANES), F32)],
        compiler_params=_params("parallel", "parallel"),
        name="fft_a",
    )(x4, x4, x4, x4, f1_tab)
    nk = n1 // SB_ROWS
    z4 = misc.reshape(batch, n2, n1, N_MISC * CB)
    out = pl.pallas_call(
        functools.partial(_fft_b_kernel, scale=1.0 / math.sqrt(seq * GDIM)),
        grid=(nk, batch),
        in_specs=[pl.BlockSpec((None, A_GROUPS, SB_ROWS, n2, LANES), lambda k, b: (b, 0, k, 0, 0)),
                  pl.BlockSpec((None, A_GROUPS, SB_ROWS, n2, LANES), lambda k, b: (b, 0, nk + k, 0, 0)),
                  pl.BlockSpec((SB_ROWS, 2 * n2, 2 * n2), lambda k, b: (k, 0, 0)),
                  pl.BlockSpec((None, n2, SB_ROWS, CB), lambda k, b: (b, 0, k, MISC_AZ)),
                  pl.BlockSpec((2 * GDIM, GDIM), lambda k, b: (0, 0)),
                  pl.BlockSpec((A_GROUPS, GDIM, GDIM), lambda k, b: (0, 0, 0))],
        out_specs=pl.BlockSpec((None, n2, SB_ROWS, CB), lambda k, b: (b, 0, k, 0)),
        out_shape=jax.ShapeDtypeStruct((batch, n2, n1, CB), BF16),
        scratch_shapes=[pltpu.VMEM((A_GROUPS, n2 * SB_ROWS, LANES), F32),
                        pltpu.VMEM((A_GROUPS, 2 * GDIM, GDIM), BF16)],
        compiler_params=_params("parallel", "parallel"),
        name="fft_b",
    )(y, y, m_tab, z4, cs_tab, wmap_bf)
    return out.reshape(batch * seq, CB)


def _dft_tables(seq):
    n2 = B_CHUNK
    n1 = seq // n2
    two_pi = 2.0 * math.pi

    def cs(idx, period):
        ang = (idx % period).astype(F32) * (two_pi / period)
        return jnp.cos(ang), jnp.sin(ang)

    i1 = jnp.arange(n1, dtype=jnp.int32)
    c1, s1 = cs(i1[:, None] * i1[None, :], n1)
    f1_tab = jnp.concatenate([c1, -s1], axis=0).astype(BF16)
    k = i1[:, None, None] + n1 * jnp.arange(n2, dtype=jnp.int32)[None, :, None]
    s2 = jnp.arange(n2, dtype=jnp.int32)[None, None, :]
    cm, sm = cs(k * s2, seq)
    m_tab = jnp.concatenate([jnp.concatenate([cm, sm], axis=2),
                             jnp.concatenate([-sm, cm], axis=2)], axis=1).astype(BF16)
    ic = jnp.arange(GDIM, dtype=jnp.int32)
    cc, sc = cs(ic[:, None] * ic[None, :], GDIM)
    cs_tab = jnp.concatenate([cc, sc], axis=0).astype(BF16)
    return f1_tab, m_tab, cs_tab


Q_SUB = 128
ATTN_ROWS = 512


def _attn_kernel(q_ref, kc_ref, kp_ref, kn_ref, vc_ref, vp_ref, vn_ref, o_ref, l_ref, *, seq_len, tq):
    t = pl.program_id(2)
    tk = Q_SUB + 2 * HALF_WIN
    ones = jnp.ones((tk, LANES), BF16)
    row = lax.broadcasted_iota(jnp.int32, (2 * Q_SUB, tk), 0) & (Q_SUB - 1)
    col = lax.broadcasted_iota(jnp.int32, (2 * Q_SUB, tk), 1)
    band = jnp.abs(row + HALF_WIN - col) <= HALF_WIN
    lo = lax.broadcasted_iota(jnp.int32, (Q_SUB, LANES), 1) < C_HEAD_DIM
    for rr in range(q_ref.shape[0]):
        k = jnp.concatenate([kp_ref[rr], kc_ref[rr], kn_ref[rr]], axis=0)
        v = jnp.concatenate([vp_ref[rr], vc_ref[rr], vn_ref[rr]], axis=0)
        for u in range(tq // Q_SUB):
            rows = slice(u * Q_SUB, (u + 1) * Q_SUB)
            kpos = t * tq + u * Q_SUB - HALF_WIN + col[:1]
            valid = band & ((kpos >= 0) & (kpos < seq_len))
            q = q_ref[rr, rows, :]
            ku, vu = k[u * Q_SUB:u * Q_SUB + tk], v[u * Q_SUB:u * Q_SUB + tk]
            for hp in range(C_HEADS // 2):
                cols = slice(hp * LANES, (hp + 1) * LANES)
                qp = q[:, cols]
                zero = jnp.zeros_like(qp)
                q2 = jnp.concatenate([jnp.where(lo, qp, zero), jnp.where(lo, zero, qp)], axis=0)
                s = lax.dot_general(q2, ku[:, cols], (((1,), (1,)), ((), ())), preferred_element_type=F32)
                s = jnp.where(valid, s, NEG_BIG)
                m = jnp.max(s, axis=1, keepdims=True)
                p = jnp.exp2(s - m).astype(BF16)
                pv = jnp.dot(p, jnp.concatenate([vu[:, cols], ones], axis=1), preferred_element_type=F32)
                den = pv[:, LANES:]
                o = pv[:, :LANES] / den
                lse = (m + jnp.log2(den)) * math.log(2.0)
                o_ref[rr, rows, cols] = jnp.where(lo, o[:Q_SUB], o[Q_SUB:]).astype(BF16)
                l_ref[rr, rows, cols] = jnp.where(lo, lse[:Q_SUB], lse[Q_SUB:])


def _attn_group(qkv, g):
    batch, dil, ln, _ = qkv.shape
    tq = min(ATTN_ROWS, ln)
    nres = min(dil, ATTN_ROWS // tq)
    per = tq // HALF_WIN
    nhalo = ln // HALF_WIN

    def cur(blk):
        return pl.BlockSpec((None, nres, tq, CB), lambda b, r, t: (b, r, t, blk))

    def before(blk):
        return pl.BlockSpec((None, nres, HALF_WIN, CB),
                            lambda b, r, t: (b, r, jnp.maximum(t * per - 1, 0), blk))

    def after(blk):
        return pl.BlockSpec((None, nres, HALF_WIN, CB),
                            lambda b, r, t: (b, r, jnp.minimum((t + 1) * per, nhalo - 1), blk))

    tile = pl.BlockSpec((None, nres, tq, CB), lambda b, r, t: (b, r, t, 0))
    return pl.pallas_call(
        functools.partial(_attn_kernel, seq_len=ln, tq=tq),
        grid=(batch, dil // nres, ln // tq),
        in_specs=[cur(0), cur(1), before(1), after(1), cur(2), before(2), after(2)],
        out_specs=[tile, tile],
        out_shape=[jax.ShapeDtypeStruct((batch, dil, ln, CB), BF16),
                   jax.ShapeDtypeStruct((batch, dil, ln, CB), F32)],
        compiler_params=_params("parallel", "parallel", "parallel"),
        name=f"attn{g}",
    )(*([qkv] * 7))


def _combine_kernel(o0_ref, l0_ref, o1_ref, l1_ref, o2_ref, l2_ref, cz_ref, y_ref, *scr):
    tm = o0_ref.shape[0]
    for (o_ref, l_ref), (os_ref, ls_ref), dil in zip(((o1_ref, l1_ref), (o2_ref, l2_ref)),
                                                      (scr[:2], scr[2:]), C_DILATIONS[1:]):
        n = tm // dil
        for r in range(dil):
            rows = pl.ds(r, n, stride=dil)
            for c in range(NSLAB):
                cols = slice(c * LANES, (c + 1) * LANES)
                os_ref[c, rows, :] = o_ref[r, :, cols].astype(F32)
                ls_ref[c, rows, :] = l_ref[r, :, cols]
    for c in range(NSLAB):
        cols = slice(c * LANES, (c + 1) * LANES)
        l0, l1, l2 = l0_ref[:, cols], scr[1][c], scr[3][c]
        mx = jnp.maximum(jnp.maximum(l0, l1), l2)
        e0, e1, e2 = jnp.exp(l0 - mx), jnp.exp(l1 - mx), jnp.exp(l2 - mx)
        num = e0 * o0_ref[:, cols].astype(F32) + e1 * scr[0][c] + e2 * scr[2][c]
        y_ref[:, cols] = (num / (e0 + e1 + e2) * cz_ref[:, cols].astype(F32)).astype(BF16)


def _combine(res, misc, batch, seq, tm):
    (o0, l0), (o1, l1), (o2, l2) = res
    nseq = seq // tm
    nat = pl.BlockSpec((None, None, tm, CB), lambda i: (i // nseq, 0, i % nseq, 0))

    def dil_spec(dil):
        return pl.BlockSpec((None, dil, tm // dil, CB), lambda i: (i // nseq, 0, i % nseq, 0))

    s1, s2 = dil_spec(C_DILATIONS[1]), dil_spec(C_DILATIONS[2])
    return pl.pallas_call(
        _combine_kernel,
        grid=(batch * nseq,),
        in_specs=[nat, nat, s1, s1, s2, s2, pl.BlockSpec((tm, CB), lambda i: (i, MISC_CZ))],
        out_specs=pl.BlockSpec((tm, CB), lambda i: (i, 0)),
        out_shape=jax.ShapeDtypeStruct((batch * seq, CB), BF16),
        scratch_shapes=[pltpu.VMEM((NSLAB, tm, LANES), F32)] * 4,
        compiler_params=_params("parallel"),
        name="combine",
    )(o0, l0, o1, l1, o2, l2, misc)


def _merge_kernel(x_ref, ya_ref, yb_ref, yc_ref, wg0_ref, wg1_ref, wg2_ref, wb_ref, o_ref, xb_ref):
    @pl.when(pl.program_id(1) == 0)
    def _():
        xb_ref[...] = x_ref[...].astype(BF16)

    xb = xb_ref[...]
    acc = None
    for kk, (y_ref, wg_ref) in enumerate(((ya_ref, wg0_ref), (yb_ref, wg1_ref), (yc_ref, wg2_ref))):
        gate = jax.nn.sigmoid(jnp.dot(xb, wg_ref[...], preferred_element_type=F32))
        term = gate * jnp.dot(y_ref[...], wb_ref[kk], preferred_element_type=F32)
        acc = term if acc is None else acc + term
    o_ref[...] = acc.astype(BF16)


def _merge(x2, ya, yb, yc, w_all, wbr_all, layer, tm):
    m = x2.shape[0]
    ncol = D_MODEL // CB
    ybs = pl.BlockSpec((tm, CB), lambda i, c: (i, 0))

    def gate_spec(kk):
        return pl.BlockSpec((None, D_MODEL, CB), lambda i, c: (layer, 0, OFF_G // CB + kk * ncol + c))

    return pl.pallas_call(
        _merge_kernel,
        grid=(m // tm, ncol),
        in_specs=[pl.BlockSpec((tm, D_MODEL), lambda i, c: (i, 0)), ybs, ybs, ybs,
                  gate_spec(0), gate_spec(1), gate_spec(2),
                  pl.BlockSpec((None, N_BRANCH, W_BR, CB), lambda i, c: (layer, 0, 0, c))],
        out_specs=pl.BlockSpec((tm, CB), lambda i, c: (i, c)),
        out_shape=jax.ShapeDtypeStruct((m, D_MODEL), BF16),
        scratch_shapes=[pltpu.VMEM((tm, D_MODEL), BF16)],
        compiler_params=_params("parallel", "arbitrary"),
        name="merge",
    )(x2, ya, yb, yc, w_all, w_all, w_all, wbr_all)


def _outproj_kernel(x_ref, m_ref, wo_ref, g_ref, b_ref, o_ref):
    for rc in range(x_ref.shape[0] // ROW_CHUNK):
        rows = slice(rc * ROW_CHUNK, (rc + 1) * ROW_CHUNK)
        out = jnp.dot(m_ref[rows, :], wo_ref[...], preferred_element_type=F32)
        z = ALPHA * x_ref[rows, :] + out
        mu = jnp.mean(z, axis=-1, keepdims=True)
        zc = z - mu
        var = jnp.mean(zc * zc, axis=-1, keepdims=True)
        o_ref[rows, :] = zc * lax.rsqrt(var + LN_EPS) * g_ref[...] + b_ref[...]


def _outproj(x2, merged, wo_all, layer, ln_g, ln_b, tm):
    m = x2.shape[0]
    row = pl.BlockSpec((tm, D_MODEL), lambda i: (i, 0))
    vec = pl.BlockSpec((1, D_MODEL), lambda i: (0, 0))
    return pl.pallas_call(
        _outproj_kernel,
        grid=(m // tm,),
        in_specs=[row, row,
                  pl.BlockSpec((None, D_MODEL, D_MODEL), lambda i: (layer, 0, 0), pipeline_mode=pl.Buffered(1)),
                  vec, vec],
        out_specs=row,
        out_shape=jax.ShapeDtypeStruct((m, D_MODEL), F32),
        compiler_params=_params("parallel"),
        name="outproj",
    )(x2, merged, wo_all, ln_g, ln_b)


def _rope_tables(seq, tm):
    half = ROPE_DIM // 2
    inv_freq = ROPE_THETA ** (-jnp.arange(half, dtype=F32) / half)
    row = jnp.arange(seq, dtype=jnp.int32)

    def tables(dil):
        in_tile = row % tm
        pos = (row - in_tile + in_tile // (tm // dil) + dil * (in_tile % (tm // dil))).astype(F32)
        ang = pos[:, None] * inv_freq[None, :]
        cos, sin = jnp.cos(ang), jnp.sin(ang)
        ones = jnp.ones((seq, C_HEAD_DIM - ROPE_DIM), F32)
        zeros = jnp.zeros((seq, C_HEAD_DIM - half), F32)
        cos_h = jnp.concatenate([cos, cos, ones], axis=1)
        lo_h = jnp.concatenate([-sin, zeros], axis=1)
        hi_h = jnp.concatenate([jnp.zeros((seq, half), F32), sin, zeros[:, :C_HEAD_DIM - ROPE_DIM]], axis=1)
        return jnp.stack([jnp.tile(t, (1, LANES // C_HEAD_DIM)) for t in (cos_h, lo_h, hi_h)])

    return jnp.stack([tables(dil) for dil in C_DILATIONS])


TM_INPROJ = 1024
TM_SGU = 512
TM_COMBINE = 1024
TM_MERGE = 1024
TM_OUTPROJ = 1024


def _layer(x2, batch, seq, wl, rope_tab, dft_t):
    h_a, misc, g0, g1, g2 = _inproj(x2, wl["w_in"], wl["layer"], rope_tab, batch, seq, TM_INPROJ)
    y_a = _fourier(h_a, misc, *dft_t, wl["w_amap"], batch, seq)
    y_b = _sgu(misc, wl["sgu_ln_g"], wl["sgu_ln_b"], wl["sgu_w"], wl["sgu_bias"], TM_SGU)
    groups = (g0.reshape(batch, 1, seq, 3 * CB), g1, g2)
    y_c = _combine([_attn_group(qkv, g) for g, qkv in enumerate(groups)], misc, batch, seq, TM_COMBINE)
    merged = _merge(x2, y_a, y_b, y_c, wl["w_in"], wl["w_branch"], wl["layer"], TM_MERGE)
    return _outproj(x2, merged, wl["w_out"], wl["layer"], wl["ln_g"], wl["ln_b"], TM_OUTPROJ)


def kernel(x_prompt, x_sample, w_in, w_amap, sgu_ln_g, sgu_ln_b, sgu_w, sgu_b, w_branch, w_out, ln_g, ln_b):
    w_in_bf, w_branch_bf, w_out_bf = w_in.astype(BF16), w_branch.astype(BF16), w_out.astype(BF16)
    layers = []
    for l in range(DEPTH):
        layers.append({
            "layer": l,
            "w_in": w_in_bf,
            "w_amap": w_amap[l].astype(BF16),
            "sgu_ln_g": sgu_ln_g[l][None, :],
            "sgu_ln_b": sgu_ln_b[l][None, :],
            "sgu_w": sgu_w[l].astype(BF16),
            "sgu_bias": jnp.repeat(sgu_b[l].T, GDIM, axis=1),
            "w_branch": w_branch_bf,
            "w_out": w_out_bf,
            "ln_g": ln_g[l][None, :],
            "ln_b": ln_b[l][None, :],
        })
    outs = []
    for x in (x_prompt, x_sample):
        batch, seq, _ = x.shape
        rope_t = _rope_tables(seq, TM_INPROJ)
        dft_t = _dft_tables(seq)
        x2 = x.reshape(batch * seq, D_MODEL)
        for wl in layers:
            x2 = _layer(x2, batch, seq, wl, rope_t, dft_t)
        outs.append(x2.reshape(batch, seq, D_MODEL))
    return tuple(outs)
```

```python
import functools
import math

import jax
import jax.numpy as jnp
from jax import lax
from jax.experimental import pallas as pl
from jax.experimental.pallas import tpu as pltpu

F32 = jnp.float32
BF16 = jnp.bfloat16

D_MODEL = 2048
DEPTH = 2
W_BR = 512
A_GROUPS = 4
GDIM = 128
B_CHUNK = 128
C_DILATIONS = (1, 4, 16)
C_NG = len(C_DILATIONS)
C_HEADS = 8
C_HEAD_DIM = 64
C_QKV = C_NG * C_HEADS * C_HEAD_DIM
ROPE_DIM = C_HEAD_DIM // 4
ROPE_THETA = 500000.0
N_BRANCH = 3
ALPHA = (2 * DEPTH) ** 0.25
LN_EPS = 1e-5
NEG_BIG = -1e30
HALF_WIN = 64
Q_SCALE = math.log2(math.e) / math.sqrt(C_HEAD_DIM)

LANES = 128
BF16_ROWS = 16
CB = 512
NSLAB = CB // LANES
OFF_A, OFF_AZ, OFF_B, OFF_BZ, OFF_C = 0, 512, 1024, 2048, 2560
OFF_CZ = OFF_C + 3 * C_QKV
OFF_G = OFF_CZ + W_BR
N_MIX_BLOCKS = 15
MISC_AZ, MISC_U, MISC_V, MISC_BZ, MISC_CZ = 0, 1, 2, 3, 4
N_MISC = 5
J_MISC = 1
J_QKV = J_MISC + N_MISC

VMEM_LIMIT = 56 * 1024 * 1024


def _params(*sem):
    return pltpu.CompilerParams(dimension_semantics=sem, vmem_limit_bytes=VMEM_LIMIT)


def _slabs(x):
    return [x[:, c * LANES:(c + 1) * LANES] for c in range(x.shape[1] // LANES)]


def _silu(h):
    return h * jax.nn.sigmoid(h)


def _gelu_tanh(h):
    c = math.sqrt(2.0 / math.pi)
    return h * (0.5 * (1.0 + jnp.tanh(c * (h + 0.044715 * (h * h * h)))))


def _rope_factors(t):
    lane = lax.broadcasted_iota(jnp.int32, t.shape, 1) & (C_HEAD_DIM - 1)
    cosf = jnp.where((lane >= ROPE_DIM) & (lane < 2 * ROPE_DIM), 1.0, t)
    sinf = jnp.where(lane < ROPE_DIM, pltpu.roll(t, LANES - ROPE_DIM, axis=1), 0.0)
    return cosf, sinf, lane < ROPE_DIM // 2


def _rope_slab(hc, cosf, sinf, first_half):
    up = pltpu.roll(hc, LANES - ROPE_DIM // 2, axis=1)
    dn = pltpu.roll(hc, ROPE_DIM // 2, axis=1)
    return hc * cosf + jnp.where(first_half, up, dn) * sinf


ROW_CHUNK = 256


def _inproj_kernel(x_ref, w_ref, tab_ref, ha_ref, misc_ref, g0_ref, g1_ref, g2_ref, tabs_ref, acc_ref,
                   *xb_scratch):
    j = pl.program_id(1)
    tm = x_ref.shape[0]
    lhs_ref = xb_scratch[0] if xb_scratch else x_ref

    @pl.when(j == 0)
    def _():
        if xb_scratch:
            lhs_ref[...] = x_ref[...].astype(BF16)
        tabs_ref[0] = tab_ref[...]
        for g, dil in enumerate(C_DILATIONS[1:], start=1):
            n = tm // dil
            for r in range(dil):
                tabs_ref[g, r * n:(r + 1) * n, :] = tab_ref[pl.ds(r, n, stride=dil), :]

    def chunks(size=ROW_CHUNK):
        for rc in range(tm // size):
            rows = slice(rc * size, (rc + 1) * size)
            yield rc, rows, jnp.dot(lhs_ref[rows, :], w_ref[...], preferred_element_type=F32)

    @pl.when(j == 0)
    def _():
        for _, rows, res in chunks():
            ha_ref[rows, :] = res

    @pl.when((j == J_MISC + MISC_AZ) | (j == J_MISC + MISC_BZ) | (j == J_MISC + MISC_CZ))
    def _():
        for _, rows, res in chunks():
            misc_ref[rows, :] = _silu(res).astype(BF16)

    @pl.when((j == J_MISC + MISC_U) | (j == J_MISC + MISC_V))
    def _():
        for _, rows, res in chunks():
            misc_ref[rows, :] = _gelu_tanh(res).astype(BF16)

    for g, (dil, out_ref) in enumerate(zip(C_DILATIONS, (g0_ref, g1_ref, g2_ref))):
        for part in range(3):
            @pl.when(j == J_QKV + 3 * g + part)
            def _(g=g, dil=dil, out_ref=out_ref, rotate=part < 2, is_q=part == 0):
                size = max(ROW_CHUNK, BF16_ROWS * dil)
                n = size // dil
                for rc, rows, res in chunks(size):
                    if dil > 1:
                        for c, slab in enumerate(_slabs(res)):
                            acc_ref[rc % 2, c, :size, :] = slab
                    for r in range(dil):
                        if rotate:
                            factors = _rope_factors(tabs_ref[g, pl.ds(r * (tm // dil) + rc * n, n), :])
                        for c in range(NSLAB):
                            cols = slice(c * LANES, (c + 1) * LANES)
                            if dil > 1:
                                hc = acc_ref[rc % 2, c, pl.ds(r, n, stride=dil), :]
                            else:
                                hc = res[:, cols]
                            if rotate:
                                hc = _rope_slab(hc, *factors)
                            if is_q:
                                hc = hc * Q_SCALE
                            if dil > 1:
                                out_ref[r, rc * n:(rc + 1) * n, cols] = hc.astype(BF16)
                            else:
                                out_ref[rows, cols] = hc.astype(BF16)


MIX_BLOCK_OF = tuple(off // CB for off in (
    [OFF_A, OFF_AZ, OFF_B, OFF_B + CB, OFF_BZ, OFF_CZ]
    + [OFF_C + part * C_QKV + g * CB for g in range(C_NG) for part in range(3)]))


def _mix_block(j):
    blk = jnp.int32(MIX_BLOCK_OF[0])
    for step in range(1, N_MIX_BLOCKS):
        blk = jnp.where(j == step, MIX_BLOCK_OF[step], blk)
    return blk


def _inproj(x2, w_all, layer, rope_tab, batch, seq, tm):
    m = x2.shape[0]
    nseq = seq // tm

    def part(j, first):
        return jnp.clip(j - first, 0, 2)

    def dil_spec(g):
        dil = C_DILATIONS[g]
        return pl.BlockSpec((None, dil, tm // dil, CB),
                            lambda i, j: (i // nseq, 0, i % nseq, part(j, J_QKV + 3 * g)))

    def dil_shape(g):
        dil = C_DILATIONS[g]
        return jax.ShapeDtypeStruct((batch, dil, seq // dil, 3 * CB), BF16)

    return pl.pallas_call(
        _inproj_kernel,
        grid=(m // tm, N_MIX_BLOCKS),
        in_specs=[
            pl.BlockSpec((tm, D_MODEL), lambda i, j: (i, 0)),
            pl.BlockSpec((None, D_MODEL, CB), lambda i, j: (layer, 0, _mix_block(j))),
            pl.BlockSpec((tm, LANES), lambda i, j: (i % nseq, 0)),
        ],
        out_specs=[
            pl.BlockSpec((tm, CB), lambda i, j: (i, 0)),
            pl.BlockSpec((tm, CB), lambda i, j: (i, jnp.clip(j - J_MISC, 0, N_MISC - 1))),
            pl.BlockSpec((tm, CB), lambda i, j: (i, part(j, J_QKV))),
            dil_spec(1), dil_spec(2),
        ],
        out_shape=[
            jax.ShapeDtypeStruct((m, CB), F32),
            jax.ShapeDtypeStruct((m, N_MISC * CB), BF16),
            jax.ShapeDtypeStruct((m, 3 * CB), BF16),
            dil_shape(1), dil_shape(2),
        ],
        scratch_shapes=[pltpu.VMEM((C_NG, tm, LANES), F32), pltpu.VMEM((2, NSLAB, ROW_CHUNK, LANES), F32)]
        + ([] if x2.dtype == BF16 else [pltpu.VMEM((tm, D_MODEL), BF16)]),
        compiler_params=_params("parallel", "arbitrary"),
        name="inproj",
    )(x2, w_all, rope_tab)


def _sgu_kernel(u_ref, v_ref, z_ref, g_ref, b_ref, ws_ref, bs_ref, o_ref):
    tm = u_ref.shape[0]
    v = v_ref[...].astype(F32)
    mu = jnp.mean(v, axis=-1, keepdims=True)
    vc = v - mu
    var = jnp.mean(vc * vc, axis=-1, keepdims=True)
    vn = (vc * lax.rsqrt(var + LN_EPS) * g_ref[...] + b_ref[...]).astype(BF16)
    bias = bs_ref[...]
    nchunk = tm // B_CHUNK
    mixed = []
    for g in range(A_GROUPS):
        cols = slice(g * GDIM, (g + 1) * GDIM)
        side = jnp.concatenate([vn[n * B_CHUNK:(n + 1) * B_CHUNK, cols] for n in range(nchunk)], axis=1)
        mixed.append(jnp.dot(ws_ref[g], side, preferred_element_type=F32))
    for n in range(nchunk):
        rows = slice(n * B_CHUNK, (n + 1) * B_CHUNK)
        mix = jnp.concatenate([mixed[g][:, n * GDIM:(n + 1) * GDIM] for g in range(A_GROUPS)], axis=1) + bias
        o_ref[rows, :] = (u_ref[rows, :].astype(F32) * mix * z_ref[rows, :].astype(F32)).astype(BF16)


def _sgu(misc, ln_g, ln_b, ws_bf, bias_full, tm):
    m = misc.shape[0]
    col = lambda blk: pl.BlockSpec((tm, CB), lambda i: (i, blk))
    full = lambda shape: pl.BlockSpec(shape, lambda i: (0,) * len(shape))
    return pl.pallas_call(
        _sgu_kernel,
        grid=(m // tm,),
        in_specs=[col(MISC_U), col(MISC_V), col(MISC_BZ), full((1, CB)), full((1, CB)),
                  full((A_GROUPS, B_CHUNK, B_CHUNK)), full((B_CHUNK, CB))],
        out_specs=pl.BlockSpec((tm, CB), lambda i: (i, 0)),
        out_shape=jax.ShapeDtypeStruct((m, CB), BF16),
        compiler_params=_params("parallel"),
        name="sgu",
    )(misc, misc, misc, ln_g, ln_b, ws_bf, bias_full)


SA_ROWS = 8
SA_BLOCKS = 4
SB_ROWS = BF16_ROWS


def _fft_a_kernel(x0_ref, x1_ref, x2_ref, x3_ref, f_ref, y_ref, xin_ref, yout_ref):
    n1 = x0_ref.shape[0]
    for blk in range(SA_BLOCKS):
        srows = slice(blk * SA_ROWS, (blk + 1) * SA_ROWS)
        for g, x_ref in enumerate((x0_ref, x1_ref, x2_ref, x3_ref)):
            xin_ref[blk, g] = x_ref[:, srows, :].reshape(n1 * SA_ROWS, LANES)
        for jj in range(SA_ROWS):
            xj = jnp.concatenate(
                [xin_ref[blk, g, pl.ds(jj, n1, stride=SA_ROWS), :] for g in range(A_GROUPS)],
                axis=1).astype(BF16)
            yj = jnp.dot(f_ref[...], xj, preferred_element_type=F32)
            for g, slab in enumerate(_slabs(yj)):
                yout_ref[blk, g, pl.ds(jj, 2 * n1, stride=SA_ROWS), :] = slab
        for g in range(A_GROUPS):
            y_ref[g, :, srows, :] = yout_ref[blk, g].reshape(2 * n1, SA_ROWS, LANES)


def _fft_b_kernel(yr_ref, yi_ref, m_ref, z_ref, cs_ref, wm_ref, o_ref, f_ref, csw_ref, *, scale):
    n2 = yr_ref.shape[2]
    for g in range(A_GROUPS):
        csw_ref[g] = (jnp.dot(cs_ref[...], wm_ref[g], preferred_element_type=F32) * scale).astype(BF16)

    def body(kk, carry):
        ycat = jnp.concatenate(
            [jnp.concatenate([y_ref[g, kk] for g in range(A_GROUPS)], axis=1) for y_ref in (yr_ref, yi_ref)],
            axis=0).astype(BF16)
        zz = jnp.dot(m_ref[kk], ycat, preferred_element_type=F32).astype(BF16)
        for g in range(A_GROUPS):
            cols = slice(g * GDIM, (g + 1) * GDIM)
            z_ri = jnp.concatenate([zz[:n2, cols], zz[n2:, cols]], axis=1)
            f_ref[g, pl.ds(kk, n2, stride=SB_ROWS), :] = jnp.dot(
                z_ri, csw_ref[g], preferred_element_type=F32)
        return carry

    lax.fori_loop(0, SB_ROWS, body, 0)
    mixed = jnp.concatenate([f_ref[g] for g in range(A_GROUPS)], axis=1)
    gate = z_ref[...].reshape(n2 * SB_ROWS, CB).astype(F32)
    o_ref[...] = (mixed * gate).astype(BF16).reshape(n2, SB_ROWS, CB)


def _fourier(h_a, misc, f1_tab, m_tab, cs_tab, wmap_bf, batch, seq):
    n2 = B_CHUNK
    n1 = seq // n2
    x4 = h_a.reshape(batch, n1, n2, CB)
    step_rows = SA_ROWS * SA_BLOCKS
    xspec = lambda g: pl.BlockSpec((None, n1, step_rows, LANES), lambda b, t: (b, 0, t, g))
    y = pl.pallas_call(
        _fft_a_kernel,
        grid=(batch, n2 // step_rows),
        in_specs=[xspec(0), xspec(1), xspec(2), xspec(3),
                  pl.BlockSpec((2 * n1, n1), lambda b, t: (0, 0))],
        out_specs=pl.BlockSpec((None, A_GROUPS, 2 * n1, step_rows, LANES), lambda b, t: (b, 0, 0, t, 0)),
        out_shape=jax.ShapeDtypeStruct((batch, A_GROUPS, 2 * n1, n2, LANES), F32),
        scratch_shapes=[pltpu.VMEM((SA_BLOCKS, A_GROUPS, n1 * SA_ROWS, LANES), F32),
                        pltpu.VMEM((SA_BLOCKS, A_GROUPS, 2 * n1 * SA_ROWS, LANES), F32)],
        compiler_params=_params("parallel", "parallel"),
        name="fft_a",
    )(x4, x4, x4, x4, f1_tab)
    nk = n1 // SB_ROWS
    z4 = misc.reshape(batch, n2, n1, N_MISC * CB)
    out = pl.pallas_call(
        functools.partial(_fft_b_kernel, scale=1.0 / math.sqrt(seq * GDIM)),
        grid=(nk, batch),
        in_specs=[pl.BlockSpec((None, A_GROUPS, SB_ROWS, n2, LANES), lambda k, b: (b, 0, k, 0, 0)),
                  pl.BlockSpec((None, A_GROUPS, SB_ROWS, n2, LANES), lambda k, b: (b, 0, nk + k, 0, 0)),
                  pl.BlockSpec((SB_ROWS, 2 * n2, 2 * n2), lambda k, b: (k, 0, 0)),
                  pl.BlockSpec((None, n2, SB_ROWS, CB), lambda k, b: (b, 0, k, MISC_AZ)),
                  pl.BlockSpec((2 * GDIM, GDIM), lambda k, b: (0, 0)),
                  pl.BlockSpec((A_GROUPS, GDIM, GDIM), lambda k, b: (0, 0, 0))],
        out_specs=pl.BlockSpec((None, n2, SB_ROWS, CB), lambda k, b: (b, 0, k, 0)),
        out_shape=jax.ShapeDtypeStruct((batch, n2, n1, CB), BF16),
        scratch_shapes=[pltpu.VMEM((A_GROUPS, n2 * SB_ROWS, LANES), F32),
                        pltpu.VMEM((A_GROUPS, 2 * GDIM, GDIM), BF16)],
        compiler_params=_params("parallel", "parallel"),
        name="fft_b",
    )(y, y, m_tab, z4, cs_tab, wmap_bf)
    return out.reshape(batch * seq, CB)


def _dft_tables(seq):
    n2 = B_CHUNK
    n1 = seq // n2
    two_pi = 2.0 * math.pi

    def cs(idx, period):
        ang = (idx % period).astype(F32) * (two_pi / period)
        return jnp.cos(ang), jnp.sin(ang)

    i1 = jnp.arange(n1, dtype=jnp.int32)
    c1, s1 = cs(i1[:, None] * i1[None, :], n1)
    f1_tab = jnp.concatenate([c1, -s1], axis=0).astype(BF16)
    k = i1[:, None, None] + n1 * jnp.arange(n2, dtype=jnp.int32)[None, :, None]
    s2 = jnp.arange(n2, dtype=jnp.int32)[None, None, :]
    cm, sm = cs(k * s2, seq)
    m_tab = jnp.concatenate([jnp.concatenate([cm, sm], axis=2),
                             jnp.concatenate([-sm, cm], axis=2)], axis=1).astype(BF16)
    ic = jnp.arange(GDIM, dtype=jnp.int32)
    cc, sc = cs(ic[:, None] * ic[None, :], GDIM)
    cs_tab = jnp.concatenate([cc, sc], axis=0).astype(BF16)
    return f1_tab, m_tab, cs_tab


Q_SUB = 128
ATTN_ROWS = 512


def _attn_kernel(q_ref, kc_ref, kp_ref, kn_ref, vc_ref, vp_ref, vn_ref, o_ref, l_ref, *, seq_len, tq):
    t = pl.program_id(2)
    tk = Q_SUB + 2 * HALF_WIN
    ones = jnp.ones((tk, LANES), BF16)
    row = lax.broadcasted_iota(jnp.int32, (2 * Q_SUB, tk), 0) & (Q_SUB - 1)
    col = lax.broadcasted_iota(jnp.int32, (2 * Q_SUB, tk), 1)
    band = jnp.abs(row + HALF_WIN - col) <= HALF_WIN
    lo = lax.broadcasted_iota(jnp.int32, (Q_SUB, LANES), 1) < C_HEAD_DIM
    for rr in range(q_ref.shape[0]):
        k = jnp.concatenate([kp_ref[rr], kc_ref[rr], kn_ref[rr]], axis=0)
        v = jnp.concatenate([vp_ref[rr], vc_ref[rr], vn_ref[rr]], axis=0)
        for u in range(tq // Q_SUB):
            rows = slice(u * Q_SUB, (u + 1) * Q_SUB)
            kpos = t * tq + u * Q_SUB - HALF_WIN + col[:1]
            valid = band & ((kpos >= 0) & (kpos < seq_len))
            q = q_ref[rr, rows, :]
            ku, vu = k[u * Q_SUB:u * Q_SUB + tk], v[u * Q_SUB:u * Q_SUB + tk]
            for hp in range(C_HEADS // 2):
                cols = slice(hp * LANES, (hp + 1) * LANES)
                qp = q[:, cols]
                zero = jnp.zeros_like(qp)
                q2 = jnp.concatenate([jnp.where(lo, qp, zero), jnp.where(lo, zero, qp)], axis=0)
                s = lax.dot_general(q2, ku[:, cols], (((1,), (1,)), ((), ())), preferred_element_type=F32)
                s = jnp.where(valid, s, NEG_BIG)
                m = jnp.max(s, axis=1, keepdims=True)
                p = jnp.exp2(s - m).astype(BF16)
                pv = jnp.dot(p, jnp.concatenate([vu[:, cols], ones], axis=1), preferred_element_type=F32)
                den = pv[:, LANES:]
                o = pv[:, :LANES] / den
                lse = m * math.log(2.0) + jnp.log(den)
                o_ref[rr, rows, cols] = jnp.where(lo, o[:Q_SUB], o[Q_SUB:]).astype(BF16)
                l_ref[rr, rows, cols] = jnp.where(lo, lse[:Q_SUB], lse[Q_SUB:])


def _attn_group(qkv, g):
    batch, dil, ln, _ = qkv.shape
    tq = min(ATTN_ROWS, ln)
    nres = min(dil, ATTN_ROWS // tq)
    per = tq // HALF_WIN
    nhalo = ln // HALF_WIN

    def cur(blk):
        return pl.BlockSpec((None, nres, tq, CB), lambda b, r, t: (b, r, t, blk))

    def before(blk):
        return pl.BlockSpec((None, nres, HALF_WIN, CB),
                            lambda b, r, t: (b, r, jnp.maximum(t * per - 1, 0), blk))

    def after(blk):
        return pl.BlockSpec((None, nres, HALF_WIN, CB),
                            lambda b, r, t: (b, r, jnp.minimum((t + 1) * per, nhalo - 1), blk))

    tile = pl.BlockSpec((None, nres, tq, CB), lambda b, r, t: (b, r, t, 0))
    return pl.pallas_call(
        functools.partial(_attn_kernel, seq_len=ln, tq=tq),
        grid=(batch, dil // nres, ln // tq),
        in_specs=[cur(0), cur(1), before(1), after(1), cur(2), before(2), after(2)],
        out_specs=[tile, tile],
        out_shape=[jax.ShapeDtypeStruct((batch, dil, ln, CB), BF16),
                   jax.ShapeDtypeStruct((batch, dil, ln, CB), F32)],
        compiler_params=_params("parallel", "parallel", "parallel"),
        name=f"attn{g}",
    )(*([qkv] * 7))


def _combine_kernel(o0_ref, l0_ref, o1_ref, l1_ref, o2_ref, l2_ref, cz_ref, y_ref, *scr):
    tm = o0_ref.shape[0]
    for (o_ref, l_ref), (os_ref, ls_ref), dil in zip(((o1_ref, l1_ref), (o2_ref, l2_ref)),
                                                      (scr[:2], scr[2:]), C_DILATIONS[1:]):
        n = tm // dil
        for r in range(dil):
            rows = pl.ds(r, n, stride=dil)
            for c in range(NSLAB):
                cols = slice(c * LANES, (c + 1) * LANES)
                os_ref[c, rows, :] = o_ref[r, :, cols].astype(F32)
                ls_ref[c, rows, :] = l_ref[r, :, cols]
    for c in range(NSLAB):
        cols = slice(c * LANES, (c + 1) * LANES)
        l0, l1, l2 = l0_ref[:, cols], scr[1][c], scr[3][c]
        mx = jnp.maximum(jnp.maximum(l0, l1), l2)
        e0, e1, e2 = jnp.exp(l0 - mx), jnp.exp(l1 - mx), jnp.exp(l2 - mx)
        num = e0 * o0_ref[:, cols].astype(F32) + e1 * scr[0][c] + e2 * scr[2][c]
        y_ref[:, cols] = (num / (e0 + e1 + e2) * cz_ref[:, cols].astype(F32)).astype(BF16)


def _combine(res, misc, batch, seq, tm):
    (o0, l0), (o1, l1), (o2, l2) = res
    nseq = seq // tm
    nat = pl.BlockSpec((None, None, tm, CB), lambda i: (i // nseq, 0, i % nseq, 0))

    def dil_spec(dil):
        return pl.BlockSpec((None, dil, tm // dil, CB), lambda i: (i // nseq, 0, i % nseq, 0))

    s1, s2 = dil_spec(C_DILATIONS[1]), dil_spec(C_DILATIONS[2])
    return pl.pallas_call(
        _combine_kernel,
        grid=(batch * nseq,),
        in_specs=[nat, nat, s1, s1, s2, s2, pl.BlockSpec((tm, CB), lambda i: (i, MISC_CZ))],
        out_specs=pl.BlockSpec((tm, CB), lambda i: (i, 0)),
        out_shape=jax.ShapeDtypeStruct((batch * seq, CB), BF16),
        scratch_shapes=[pltpu.VMEM((NSLAB, tm, LANES), F32)] * 4,
        compiler_params=_params("parallel"),
        name="combine",
    )(o0, l0, o1, l1, o2, l2, misc)


def _merge_kernel(x_ref, ya_ref, yb_ref, yc_ref, wg0_ref, wg1_ref, wg2_ref, wb_ref, o_ref, *xb_scratch):
    if xb_scratch:
        @pl.when(pl.program_id(1) == 0)
        def _():
            xb_scratch[0][...] = x_ref[...].astype(BF16)

    xb = xb_scratch[0][...] if xb_scratch else x_ref[...]
    acc = None
    for kk, (y_ref, wg_ref) in enumerate(((ya_ref, wg0_ref), (yb_ref, wg1_ref), (yc_ref, wg2_ref))):
        gate = jax.nn.sigmoid(jnp.dot(xb, wg_ref[...], preferred_element_type=F32))
        term = gate * jnp.dot(y_ref[...], wb_ref[kk], preferred_element_type=F32)
        acc = term if acc is None else acc + term
    o_ref[...] = acc.astype(BF16)


def _merge(x2, ya, yb, yc, w_all, wbr_all, layer, tm):
    m = x2.shape[0]
    ncol = D_MODEL // CB
    ybs = pl.BlockSpec((tm, CB), lambda i, c: (i, 0))

    def gate_spec(kk):
        return pl.BlockSpec((None, D_MODEL, CB), lambda i, c: (layer, 0, OFF_G // CB + kk * ncol + c))

    return pl.pallas_call(
        _merge_kernel,
        grid=(m // tm, ncol),
        in_specs=[pl.BlockSpec((tm, D_MODEL), lambda i, c: (i, 0)), ybs, ybs, ybs,
                  gate_spec(0), gate_spec(1), gate_spec(2),
                  pl.BlockSpec((None, N_BRANCH, W_BR, CB), lambda i, c: (layer, 0, 0, c))],
        out_specs=pl.BlockSpec((tm, CB), lambda i, c: (i, c)),
        out_shape=jax.ShapeDtypeStruct((m, D_MODEL), BF16),
        scratch_shapes=[] if x2.dtype == BF16 else [pltpu.VMEM((tm, D_MODEL), BF16)],
        compiler_params=_params("parallel", "arbitrary"),
        name="merge",
    )(x2, ya, yb, yc, w_all, w_all, w_all, wbr_all)


def _outproj_kernel(x_ref, m_ref, wo_ref, g_ref, b_ref, o_ref, *ob_ref):
    for rc in range(x_ref.shape[0] // ROW_CHUNK):
        rows = slice(rc * ROW_CHUNK, (rc + 1) * ROW_CHUNK)
        out = jnp.dot(m_ref[rows, :], wo_ref[...], preferred_element_type=F32)
        z = ALPHA * x_ref[rows, :] + out
        mu = jnp.mean(z, axis=-1, keepdims=True)
        zc = z - mu
        var = jnp.mean(zc * zc, axis=-1, keepdims=True)
        y = zc * lax.rsqrt(var + LN_EPS) * g_ref[...] + b_ref[...]
        o_ref[rows, :] = y
        if ob_ref:
            ob_ref[0][rows, :] = y.astype(BF16)


def _outproj(x2, merged, wo_all, layer, ln_g, ln_b, tm, with_bf16):
    m = x2.shape[0]
    row = pl.BlockSpec((tm, D_MODEL), lambda i: (i, 0))
    vec = pl.BlockSpec((1, D_MODEL), lambda i: (0, 0))
    return pl.pallas_call(
        _outproj_kernel,
        grid=(m // tm,),
        in_specs=[row, row,
                  pl.BlockSpec((None, D_MODEL, D_MODEL), lambda i: (layer, 0, 0), pipeline_mode=pl.Buffered(1)),
                  vec, vec],
        out_specs=[row, row] if with_bf16 else row,
        out_shape=([jax.ShapeDtypeStruct((m, D_MODEL), F32), jax.ShapeDtypeStruct((m, D_MODEL), BF16)]
                   if with_bf16 else jax.ShapeDtypeStruct((m, D_MODEL), F32)),
        compiler_params=_params("parallel"),
        name="outproj",
    )(x2, merged, wo_all, ln_g, ln_b)


def _rope_table(seq):
    half = ROPE_DIM // 2
    inv_freq = ROPE_THETA ** (-jnp.arange(half, dtype=F32) / half)
    ang = jnp.arange(seq, dtype=F32)[:, None] * inv_freq[None, :]
    cos, sin = jnp.cos(ang), jnp.sin(ang)
    ones = jnp.ones((seq, C_HEAD_DIM - 2 * ROPE_DIM), F32)
    head = jnp.concatenate([cos, cos, -sin, sin, ones], axis=1)
    return jnp.tile(head, (1, LANES // C_HEAD_DIM))


TM_INPROJ = 1024
TM_SGU = 512
TM_COMBINE = 1024
TM_MERGE = 1024
TM_OUTPROJ = 512


def _layer(x2, x_mm, batch, seq, wl, rope_tab, dft_t):
    h_a, misc, g0, g1, g2 = _inproj(x_mm, wl["w_in"], wl["layer"], rope_tab, batch, seq, TM_INPROJ)
    y_a = _fourier(h_a, misc, *dft_t, wl["w_amap"], batch, seq)
    y_b = _sgu(misc, wl["sgu_ln_g"], wl["sgu_ln_b"], wl["sgu_w"], wl["sgu_bias"], TM_SGU)
    groups = (g0.reshape(batch, 1, seq, 3 * CB), g1, g2)
    y_c = _combine([_attn_group(qkv, g) for g, qkv in enumerate(groups)], misc, batch, seq, TM_COMBINE)
    merged = _merge(x_mm, y_a, y_b, y_c, wl["w_in"], wl["w_branch"], wl["layer"], TM_MERGE)
    last = wl["layer"] == DEPTH - 1
    res = _outproj(x2, merged, wl["w_out"], wl["layer"], wl["ln_g"], wl["ln_b"], TM_OUTPROJ, not last)
    return (res, res) if last else tuple(res)


def kernel(x_prompt, x_sample, w_in, w_amap, sgu_ln_g, sgu_ln_b, sgu_w, sgu_b, w_branch, w_out, ln_g, ln_b):
    w_in_bf, w_branch_bf, w_out_bf = w_in.astype(BF16), w_branch.astype(BF16), w_out.astype(BF16)
    layers = []
    for l in range(DEPTH):
        layers.append({
            "layer": l,
            "w_in": w_in_bf,
            "w_amap": w_amap[l].astype(BF16),
            "sgu_ln_g": sgu_ln_g[l][None, :],
            "sgu_ln_b": sgu_ln_b[l][None, :],
            "sgu_w": sgu_w[l].astype(BF16),
            "sgu_bias": jnp.repeat(sgu_b[l].T, GDIM, axis=1),
            "w_branch": w_branch_bf,
            "w_out": w_out_bf,
            "ln_g": ln_g[l][None, :],
            "ln_b": ln_b[l][None, :],
        })
    outs = []
    for x in (x_prompt, x_sample):
        batch, seq, _ = x.shape
        rope_t = _rope_table(seq)
        dft_t = _dft_tables(seq)
        x2 = x_mm = x.reshape(batch * seq, D_MODEL)
        for wl in layers:
            x2, x_mm = _layer(x2, x_mm, batch, seq, wl, rope_t, dft_t)
        outs.append(x2.reshape(batch, seq, D_MODEL))
    return tuple(outs)
```

```python
import functools
import math

import jax
import jax.numpy as jnp
from jax import lax
from jax.experimental import pallas as pl
from jax.experimental.pallas import tpu as pltpu

F32 = jnp.float32
BF16 = jnp.bfloat16

D_MODEL = 2048
DEPTH = 2
W_BR = 512
A_GROUPS = 4
GDIM = 128
B_CHUNK = 128
C_DILATIONS = (1, 4, 16)
C_NG = len(C_DILATIONS)
C_HEADS = 8
C_HEAD_DIM = 64
C_QKV = C_NG * C_HEADS * C_HEAD_DIM
ROPE_DIM = C_HEAD_DIM // 4
ROPE_THETA = 500000.0
N_BRANCH = 3
ALPHA = (2 * DEPTH) ** 0.25
LN_EPS = 1e-5
NEG_BIG = -1e30
HALF_WIN = 64
Q_SCALE = math.log2(math.e) / math.sqrt(C_HEAD_DIM)

LANES = 128
BF16_ROWS = 16
CB = 512
NSLAB = CB // LANES
OFF_A, OFF_AZ, OFF_B, OFF_BZ, OFF_C = 0, 512, 1024, 2048, 2560
OFF_CZ = OFF_C + 3 * C_QKV
OFF_G = OFF_CZ + W_BR
N_MIX_BLOCKS = 15
MISC_AZ, MISC_BZ, MISC_CZ, MISC_U, MISC_V = 0, 1, 2, 3, 4
N_MISC = 5
J_MISC = 1
J_QKV = J_MISC + N_MISC

VMEM_LIMIT = 56 * 1024 * 1024


def _params(*sem):
    return pltpu.CompilerParams(dimension_semantics=sem, vmem_limit_bytes=VMEM_LIMIT)


def _slabs(x):
    return [x[:, c * LANES:(c + 1) * LANES] for c in range(x.shape[1] // LANES)]


def _silu(h):
    return h * jax.nn.sigmoid(h)


def _gelu_tanh(h):
    c = math.sqrt(2.0 / math.pi)
    return h * (0.5 * (1.0 + jnp.tanh(c * (h + 0.044715 * (h * h * h)))))


def _rope_factors(t):
    lane = lax.broadcasted_iota(jnp.int32, t.shape, 1) & (C_HEAD_DIM - 1)
    cosf = jnp.where((lane >= ROPE_DIM) & (lane < 2 * ROPE_DIM), 1.0, t)
    sinf = jnp.where(lane < ROPE_DIM, pltpu.roll(t, LANES - ROPE_DIM, axis=1), 0.0)
    return cosf, sinf, lane < ROPE_DIM // 2


def _rope_slab(hc, cosf, sinf, first_half):
    up = pltpu.roll(hc, LANES - ROPE_DIM // 2, axis=1)
    dn = pltpu.roll(hc, ROPE_DIM // 2, axis=1)
    return hc * cosf + jnp.where(first_half, up, dn) * sinf


ROW_CHUNK = 256


def _inproj_kernel(x_ref, w_ref, tab_ref, ha_ref, misc_ref, g0_ref, g1_ref, g2_ref, tabs_ref, acc_ref,
                   *xb_scratch):
    j = pl.program_id(1)
    tm = x_ref.shape[0]
    lhs_ref = xb_scratch[0] if xb_scratch else x_ref

    @pl.when(j == 0)
    def _():
        if xb_scratch:
            lhs_ref[...] = x_ref[...].astype(BF16)
        tabs_ref[0] = tab_ref[...]
        for g, dil in enumerate(C_DILATIONS[1:], start=1):
            n = tm // dil
            for r in range(dil):
                tabs_ref[g, r * n:(r + 1) * n, :] = tab_ref[pl.ds(r, n, stride=dil), :]

    def chunks(size=ROW_CHUNK):
        for rc in range(tm // size):
            rows = slice(rc * size, (rc + 1) * size)
            yield rc, rows, jnp.dot(lhs_ref[rows, :], w_ref[...], preferred_element_type=F32)

    def fourier_in():
        for _, rows, res in chunks():
            ha_ref[rows, :] = res

    def gate_path():
        for _, rows, res in chunks():
            misc_ref[rows, :] = _silu(res).astype(BF16)

    def gating_in():
        for _, rows, res in chunks():
            misc_ref[rows, :] = _gelu_tanh(res).astype(BF16)

    def qkv(g, part):
        dil, out_ref = C_DILATIONS[g], (g0_ref, g1_ref, g2_ref)[g]
        rotate, is_q = part < 2, part == 0
        size = max(ROW_CHUNK, BF16_ROWS * dil)
        n = size // dil
        for rc, rows, res in chunks(size):
            if dil > 1:
                for c, slab in enumerate(_slabs(res)):
                    acc_ref[rc % 2, c, :size, :] = slab
            for r in range(dil):
                if rotate:
                    factors = _rope_factors(tabs_ref[g, pl.ds(r * (tm // dil) + rc * n, n), :])
                for c in range(NSLAB):
                    cols = slice(c * LANES, (c + 1) * LANES)
                    if dil > 1:
                        hc = acc_ref[rc % 2, c, pl.ds(r, n, stride=dil), :]
                    else:
                        hc = res[:, cols]
                    if rotate:
                        hc = _rope_slab(hc, *factors)
                    if is_q:
                        hc = hc * Q_SCALE
                    if dil > 1:
                        out_ref[r, rc * n:(rc + 1) * n, cols] = hc.astype(BF16)
                    else:
                        out_ref[rows, cols] = hc.astype(BF16)

    leaves = [(0, fourier_in), (J_MISC + MISC_AZ, gate_path), (J_MISC + MISC_U, gating_in)]
    leaves += [(J_QKV + 3 * g + part, functools.partial(qkv, g, part))
               for g in range(C_NG) for part in range(3)]

    def dispatch(sub):
        if len(sub) == 1:
            sub[0][1]()
            return
        mid = len(sub) // 2
        pl.when(j < sub[mid][0])(lambda: dispatch(sub[:mid]))
        pl.when(j >= sub[mid][0])(lambda: dispatch(sub[mid:]))

    dispatch(leaves)


MIX_BLOCK_OF = tuple(off // CB for off in (
    [OFF_A, OFF_AZ, OFF_BZ, OFF_CZ, OFF_B, OFF_B + CB]
    + [OFF_C + part * C_QKV + g * CB for g in range(C_NG) for part in range(3)]))


def _mix_block(j):
    blk = jnp.int32(MIX_BLOCK_OF[0])
    for step in range(1, N_MIX_BLOCKS):
        blk = jnp.where(j == step, MIX_BLOCK_OF[step], blk)
    return blk


def _inproj(x2, w_all, layer, rope_tab, batch, seq, tm):
    m = x2.shape[0]
    nseq = seq // tm

    def part(j, first):
        return jnp.clip(j - first, 0, 2)

    def dil_spec(g):
        dil = C_DILATIONS[g]
        return pl.BlockSpec((None, dil, tm // dil, CB),
                            lambda i, j: (i // nseq, 0, i % nseq, part(j, J_QKV + 3 * g)))

    def dil_shape(g):
        dil = C_DILATIONS[g]
        return jax.ShapeDtypeStruct((batch, dil, seq // dil, 3 * CB), BF16)

    return pl.pallas_call(
        _inproj_kernel,
        grid=(m // tm, N_MIX_BLOCKS),
        in_specs=[
            pl.BlockSpec((tm, D_MODEL), lambda i, j: (i, 0)),
            pl.BlockSpec((None, D_MODEL, CB), lambda i, j: (layer, 0, _mix_block(j))),
            pl.BlockSpec((tm, LANES), lambda i, j: (i % nseq, 0)),
        ],
        out_specs=[
            pl.BlockSpec((tm, CB), lambda i, j: (i, 0)),
            pl.BlockSpec((tm, CB), lambda i, j: (i, jnp.clip(j - J_MISC, 0, N_MISC - 1))),
            pl.BlockSpec((tm, CB), lambda i, j: (i, part(j, J_QKV))),
            dil_spec(1), dil_spec(2),
        ],
        out_shape=[
            jax.ShapeDtypeStruct((m, CB), F32),
            jax.ShapeDtypeStruct((m, N_MISC * CB), BF16),
            jax.ShapeDtypeStruct((m, 3 * CB), BF16),
            dil_shape(1), dil_shape(2),
        ],
        scratch_shapes=[pltpu.VMEM((C_NG, tm, LANES), F32), pltpu.VMEM((2, NSLAB, ROW_CHUNK, LANES), F32)]
        + ([] if x2.dtype == BF16 else [pltpu.VMEM((tm, D_MODEL), BF16)]),
        compiler_params=_params("parallel", "arbitrary"),
        name="inproj",
    )(x2, w_all, rope_tab)


def _sgu_kernel(u_ref, v_ref, z_ref, g_ref, b_ref, ws_ref, bs_ref, o_ref):
    tm = u_ref.shape[0]
    v = v_ref[...].astype(F32)
    mu = jnp.mean(v, axis=-1, keepdims=True)
    vc = v - mu
    var = jnp.mean(vc * vc, axis=-1, keepdims=True)
    vn = (vc * lax.rsqrt(var + LN_EPS) * g_ref[...] + b_ref[...]).astype(BF16)
    bias = bs_ref[...]
    nchunk = tm // B_CHUNK
    mixed = []
    for g in range(A_GROUPS):
        cols = slice(g * GDIM, (g + 1) * GDIM)
        side = jnp.concatenate([vn[n * B_CHUNK:(n + 1) * B_CHUNK, cols] for n in range(nchunk)], axis=1)
        mixed.append(jnp.dot(ws_ref[g], side, preferred_element_type=F32))
    for n in range(nchunk):
        rows = slice(n * B_CHUNK, (n + 1) * B_CHUNK)
        mix = jnp.concatenate([mixed[g][:, n * GDIM:(n + 1) * GDIM] for g in range(A_GROUPS)], axis=1) + bias
        o_ref[rows, :] = (u_ref[rows, :].astype(F32) * mix * z_ref[rows, :].astype(F32)).astype(BF16)


def _sgu(misc, ln_g, ln_b, ws_bf, bias_full, tm):
    m = misc.shape[0]
    col = lambda blk: pl.BlockSpec((tm, CB), lambda i: (i, blk))
    full = lambda shape: pl.BlockSpec(shape, lambda i: (0,) * len(shape))
    return pl.pallas_call(
        _sgu_kernel,
        grid=(m // tm,),
        in_specs=[col(MISC_U), col(MISC_V), col(MISC_BZ), full((1, CB)), full((1, CB)),
                  full((A_GROUPS, B_CHUNK, B_CHUNK)), full((B_CHUNK, CB))],
        out_specs=pl.BlockSpec((tm, CB), lambda i: (i, 0)),
        out_shape=jax.ShapeDtypeStruct((m, CB), BF16),
        compiler_params=_params("parallel"),
        name="sgu",
    )(misc, misc, misc, ln_g, ln_b, ws_bf, bias_full)


SA_ROWS = 8
SA_BLOCKS = 4
SB_ROWS = BF16_ROWS


def _fft_a_kernel(x0_ref, x1_ref, x2_ref, x3_ref, f_ref, y_ref, xin_ref, yout_ref):
    n1 = x0_ref.shape[0]
    for blk in range(SA_BLOCKS):
        srows = slice(blk * SA_ROWS, (blk + 1) * SA_ROWS)
        for g, x_ref in enumerate((x0_ref, x1_ref, x2_ref, x3_ref)):
            xin_ref[blk, g] = x_ref[:, srows, :].reshape(n1 * SA_ROWS, LANES)
        for jj in range(SA_ROWS):
            xj = jnp.concatenate(
                [xin_ref[blk, g, pl.ds(jj, n1, stride=SA_ROWS), :] for g in range(A_GROUPS)],
                axis=1).astype(BF16)
            yj = jnp.dot(f_ref[...], xj, preferred_element_type=F32)
            for g, slab in enumerate(_slabs(yj)):
                yout_ref[blk, g, pl.ds(jj, 2 * n1, stride=SA_ROWS), :] = slab
        for g in range(A_GROUPS):
            y_ref[g, :, srows, :] = yout_ref[blk, g].reshape(2 * n1, SA_ROWS, LANES)


def _fft_b_kernel(yr_ref, yi_ref, m_ref, z_ref, cs_ref, wm_ref, o_ref, f_ref, csw_ref, *, scale):
    n2 = yr_ref.shape[2]
    for g in range(A_GROUPS):
        csw_ref[g] = (jnp.dot(cs_ref[...], wm_ref[g], preferred_element_type=F32) * scale).astype(BF16)

    def body(kk, carry):
        ycat = jnp.concatenate(
            [jnp.concatenate([y_ref[g, kk] for g in range(A_GROUPS)], axis=1) for y_ref in (yr_ref, yi_ref)],
            axis=0).astype(BF16)
        zz = jnp.dot(m_ref[kk], ycat, preferred_element_type=F32).astype(BF16)
        for g in range(A_GROUPS):
            cols = slice(g * GDIM, (g + 1) * GDIM)
            z_ri = jnp.concatenate([zz[:n2, cols], zz[n2:, cols]], axis=1)
            f_ref[g, pl.ds(kk, n2, stride=SB_ROWS), :] = jnp.dot(
                z_ri, csw_ref[g], preferred_element_type=F32)
        return carry

    lax.fori_loop(0, SB_ROWS, body, 0)
    mixed = jnp.concatenate([f_ref[g] for g in range(A_GROUPS)], axis=1)
    gate = z_ref[...].reshape(n2 * SB_ROWS, CB).astype(F32)
    o_ref[...] = (mixed * gate).astype(BF16).reshape(n2, SB_ROWS, CB)


def _fourier(h_a, misc, f1_tab, m_tab, cs_tab, wmap_bf, batch, seq):
    n2 = B_CHUNK
    n1 = seq // n2
    x4 = h_a.reshape(batch, n1, n2, CB)
    step_rows = SA_ROWS * SA_BLOCKS
    xspec = lambda g: pl.BlockSpec((None, n1, step_rows, LANES), lambda b, t: (b, 0, t, g))
    y = pl.pallas_call(
        _fft_a_kernel,
        grid=(batch, n2 // step_rows),
        in_specs=[xspec(0), xspec(1), xspec(2), xspec(3),
                  pl.BlockSpec((2 * n1, n1), lambda b, t: (0, 0))],
        out_specs=pl.BlockSpec((None, A_GROUPS, 2 * n1, step_rows, LANES), lambda b, t: (b, 0, 0, t, 0)),
        out_shape=jax.ShapeDtypeStruct((batch, A_GROUPS, 2 * n1, n2, LANES), F32),
        scratch_shapes=[pltpu.VMEM((SA_BLOCKS, A_GROUPS, n1 * SA_ROWS, LANES), F32),
                        pltpu.VMEM((SA_BLOCKS, A_GROUPS, 2 * n1 * SA_ROWS, LANES), F32)],
        compiler_params=_params("parallel", "parallel"),
        name="fft_a",
    )(x4, x4, x4, x4, f1_tab)
    nk = n1 // SB_ROWS
    z4 = misc.reshape(batch, n2, n1, N_MISC * CB)
    out = pl.pallas_call(
        functools.partial(_fft_b_kernel, scale=1.0 / math.sqrt(seq * GDIM)),
        grid=(nk, batch),
        in_specs=[pl.BlockSpec((None, A_GROUPS, SB_ROWS, n2, LANES), lambda k, b: (b, 0, k, 0, 0)),
                  pl.BlockSpec((None, A_GROUPS, SB_ROWS, n2, LANES), lambda k, b: (b, 0, nk + k, 0, 0)),
                  pl.BlockSpec((SB_ROWS, 2 * n2, 2 * n2), lambda k, b: (k, 0, 0)),
                  pl.BlockSpec((None, n2, SB_ROWS, CB), lambda k, b: (b, 0, k, MISC_AZ)),
                  pl.BlockSpec((2 * GDIM, GDIM), lambda k, b: (0, 0)),
                  pl.BlockSpec((A_GROUPS, GDIM, GDIM), lambda k, b: (0, 0, 0))],
        out_specs=pl.BlockSpec((None, n2, SB_ROWS, CB), lambda k, b: (b, 0, k, 0)),
        out_shape=jax.ShapeDtypeStruct((batch, n2, n1, CB), BF16),
        scratch_shapes=[pltpu.VMEM((A_GROUPS, n2 * SB_ROWS, LANES), F32),
                        pltpu.VMEM((A_GROUPS, 2 * GDIM, GDIM), BF16)],
        compiler_params=_params("parallel", "parallel"),
        name="fft_b",
    )(y, y, m_tab, z4, cs_tab, wmap_bf)
    return out.reshape(batch * seq, CB)


def _dft_tables(seq):
    n2 = B_CHUNK
    n1 = seq // n2
    two_pi = 2.0 * math.pi

    def cs(idx, period):
        ang = (idx % period).astype(F32) * (two_pi / period)
        return jnp.cos(ang), jnp.sin(ang)

    i1 = jnp.arange(n1, dtype=jnp.int32)
    c1, s1 = cs(i1[:, None] * i1[None, :], n1)
    f1_tab = jnp.concatenate([c1, -s1], axis=0).astype(BF16)
    k = i1[:, None, None] + n1 * jnp.arange(n2, dtype=jnp.int32)[None, :, None]
    s2 = jnp.arange(n2, dtype=jnp.int32)[None, None, :]
    cm, sm = cs(k * s2, seq)
    m_tab = jnp.concatenate([jnp.concatenate([cm, sm], axis=2),
                             jnp.concatenate([-sm, cm], axis=2)], axis=1).astype(BF16)
    ic = jnp.arange(GDIM, dtype=jnp.int32)
    cc, sc = cs(ic[:, None] * ic[None, :], GDIM)
    cs_tab = jnp.concatenate([cc, sc], axis=0).astype(BF16)
    return f1_tab, m_tab, cs_tab


Q_SUB = 128
ATTN_ROWS = 512


def _attn_kernel(q_ref, kc_ref, kp_ref, kn_ref, vc_ref, vp_ref, vn_ref, o_ref, l_ref, *, seq_len, tq):
    t = pl.program_id(2)
    tk = Q_SUB + 2 * HALF_WIN
    ones = jnp.ones((tk, LANES), BF16)
    row = lax.broadcasted_iota(jnp.int32, (2 * Q_SUB, tk), 0) & (Q_SUB - 1)
    col = lax.broadcasted_iota(jnp.int32, (2 * Q_SUB, tk), 1)
    band = jnp.abs(row + HALF_WIN - col) <= HALF_WIN
    lo = lax.broadcasted_iota(jnp.int32, (Q_SUB, LANES), 1) < C_HEAD_DIM
    for rr in range(q_ref.shape[0]):
        k = jnp.concatenate([kp_ref[rr], kc_ref[rr], kn_ref[rr]], axis=0)
        v = jnp.concatenate([vp_ref[rr], vc_ref[rr], vn_ref[rr]], axis=0)
        for u in range(tq // Q_SUB):
            rows = slice(u * Q_SUB, (u + 1) * Q_SUB)
            kpos = t * tq + u * Q_SUB - HALF_WIN + col[:1]
            valid = band & ((kpos >= 0) & (kpos < seq_len))
            q = q_ref[rr, rows, :]
            ku, vu = k[u * Q_SUB:u * Q_SUB + tk], v[u * Q_SUB:u * Q_SUB + tk]
            for hp in range(C_HEADS // 2):
                cols = slice(hp * LANES, (hp + 1) * LANES)
                qp = q[:, cols]
                zero = jnp.zeros_like(qp)
                q2 = jnp.concatenate([jnp.where(lo, qp, zero), jnp.where(lo, zero, qp)], axis=0)
                s = lax.dot_general(q2, ku[:, cols], (((1,), (1,)), ((), ())), preferred_element_type=F32)
                s = jnp.where(valid, s, NEG_BIG)
                m = jnp.max(s, axis=1, keepdims=True)
                p = jnp.exp2(s - m).astype(BF16)
                pv = jnp.dot(p, jnp.concatenate([vu[:, cols], ones], axis=1), preferred_element_type=F32)
                den = pv[:, LANES:]
                o = pv[:, :LANES] / den
                lse = m * math.log(2.0) + jnp.log(den)
                o_ref[rr, rows, cols] = jnp.where(lo, o[:Q_SUB], o[Q_SUB:]).astype(BF16)
                l_ref[rr, rows, cols] = jnp.where(lo, lse[:Q_SUB], lse[Q_SUB:])


def _attn_group(qkv, g):
    batch, dil, ln, _ = qkv.shape
    tq = min(ATTN_ROWS, ln)
    nres = min(dil, ATTN_ROWS // tq)
    per = tq // HALF_WIN
    nhalo = ln // HALF_WIN

    def cur(blk):
        return pl.BlockSpec((None, nres, tq, CB), lambda b, r, t: (b, r, t, blk))

    def before(blk):
        return pl.BlockSpec((None, nres, HALF_WIN, CB),
                            lambda b, r, t: (b, r, jnp.maximum(t * per - 1, 0), blk))

    def after(blk):
        return pl.BlockSpec((None, nres, HALF_WIN, CB),
                            lambda b, r, t: (b, r, jnp.minimum((t + 1) * per, nhalo - 1), blk))

    tile = pl.BlockSpec((None, nres, tq, CB), lambda b, r, t: (b, r, t, 0))
    return pl.pallas_call(
        functools.partial(_attn_kernel, seq_len=ln, tq=tq),
        grid=(batch, dil // nres, ln // tq),
        in_specs=[cur(0), cur(1), before(1), after(1), cur(2), before(2), after(2)],
        out_specs=[tile, tile],
        out_shape=[jax.ShapeDtypeStruct((batch, dil, ln, CB), BF16),
                   jax.ShapeDtypeStruct((batch, dil, ln, CB), F32)],
        compiler_params=_params("parallel", "parallel", "parallel"),
        name=f"attn{g}",
    )(*([qkv] * 7))


def _combine_kernel(o0_ref, l0_ref, o1_ref, l1_ref, o2_ref, l2_ref, cz_ref, y_ref, *scr):
    tm = o0_ref.shape[0]
    for (o_ref, l_ref), (os_ref, ls_ref), dil in zip(((o1_ref, l1_ref), (o2_ref, l2_ref)),
                                                      (scr[:2], scr[2:]), C_DILATIONS[1:]):
        n = tm // dil
        for r in range(dil):
            rows = pl.ds(r, n, stride=dil)
            for c in range(NSLAB):
                cols = slice(c * LANES, (c + 1) * LANES)
                os_ref[c, rows, :] = o_ref[r, :, cols].astype(F32)
                ls_ref[c, rows, :] = l_ref[r, :, cols]
    for c in range(NSLAB):
        cols = slice(c * LANES, (c + 1) * LANES)
        l0, l1, l2 = l0_ref[:, cols], scr[1][c], scr[3][c]
        mx = jnp.maximum(jnp.maximum(l0, l1), l2)
        e0, e1, e2 = jnp.exp(l0 - mx), jnp.exp(l1 - mx), jnp.exp(l2 - mx)
        num = e0 * o0_ref[:, cols].astype(F32) + e1 * scr[0][c] + e2 * scr[2][c]
        y_ref[:, cols] = (num / (e0 + e1 + e2) * cz_ref[:, cols].astype(F32)).astype(BF16)


def _combine(res, misc, batch, seq, tm):
    (o0, l0), (o1, l1), (o2, l2) = res
    nseq = seq // tm
    nat = pl.BlockSpec((None, None, tm, CB), lambda i: (i // nseq, 0, i % nseq, 0))

    def dil_spec(dil):
        return pl.BlockSpec((None, dil, tm // dil, CB), lambda i: (i // nseq, 0, i % nseq, 0))

    s1, s2 = dil_spec(C_DILATIONS[1]), dil_spec(C_DILATIONS[2])
    return pl.pallas_call(
        _combine_kernel,
        grid=(batch * nseq,),
        in_specs=[nat, nat, s1, s1, s2, s2, pl.BlockSpec((tm, CB), lambda i: (i, MISC_CZ))],
        out_specs=pl.BlockSpec((tm, CB), lambda i: (i, 0)),
        out_shape=jax.ShapeDtypeStruct((batch * seq, CB), BF16),
        scratch_shapes=[pltpu.VMEM((NSLAB, tm, LANES), F32)] * 4,
        compiler_params=_params("parallel"),
        name="combine",
    )(o0, l0, o1, l1, o2, l2, misc)


def _merge_kernel(x_ref, ya_ref, yb_ref, yc_ref, wg0_ref, wg1_ref, wg2_ref, wb_ref, o_ref, *xb_scratch):
    if xb_scratch:
        @pl.when(pl.program_id(1) == 0)
        def _():
            xb_scratch[0][...] = x_ref[...].astype(BF16)

    xb = xb_scratch[0][...] if xb_scratch else x_ref[...]
    acc = None
    for kk, (y_ref, wg_ref) in enumerate(((ya_ref, wg0_ref), (yb_ref, wg1_ref), (yc_ref, wg2_ref))):
        gate = jax.nn.sigmoid(jnp.dot(xb, wg_ref[...], preferred_element_type=F32))
        term = gate * jnp.dot(y_ref[...], wb_ref[kk], preferred_element_type=F32)
        acc = term if acc is None else acc + term
    o_ref[...] = acc.astype(BF16)


def _merge(x2, ya, yb, yc, w_all, wbr_all, layer, tm):
    m = x2.shape[0]
    ncol = D_MODEL // CB
    ybs = pl.BlockSpec((tm, CB), lambda i, c: (i, 0))

    def gate_spec(kk):
        return pl.BlockSpec((None, D_MODEL, CB), lambda i, c: (layer, 0, OFF_G // CB + kk * ncol + c))

    return pl.pallas_call(
        _merge_kernel,
        grid=(m // tm, ncol),
        in_specs=[pl.BlockSpec((tm, D_MODEL), lambda i, c: (i, 0)), ybs, ybs, ybs,
                  gate_spec(0), gate_spec(1), gate_spec(2),
                  pl.BlockSpec((None, N_BRANCH, W_BR, CB), lambda i, c: (layer, 0, 0, c))],
        out_specs=pl.BlockSpec((tm, CB), lambda i, c: (i, c)),
        out_shape=jax.ShapeDtypeStruct((m, D_MODEL), BF16),
        scratch_shapes=[] if x2.dtype == BF16 else [pltpu.VMEM((tm, D_MODEL), BF16)],
        compiler_params=_params("parallel", "arbitrary"),
        name="merge",
    )(x2, ya, yb, yc, w_all, w_all, w_all, wbr_all)


def _outproj_kernel(x_ref, m_ref, wo_ref, g_ref, b_ref, o_ref, *ob_ref):
    for rc in range(x_ref.shape[0] // ROW_CHUNK):
        rows = slice(rc * ROW_CHUNK, (rc + 1) * ROW_CHUNK)
        out = jnp.dot(m_ref[rows, :], wo_ref[...], preferred_element_type=F32)
        z = ALPHA * x_ref[rows, :] + out
        mu = jnp.mean(z, axis=-1, keepdims=True)
        zc = z - mu
        var = jnp.mean(zc * zc, axis=-1, keepdims=True)
        y = zc * lax.rsqrt(var + LN_EPS) * g_ref[...] + b_ref[...]
        o_ref[rows, :] = y
        if ob_ref:
            ob_ref[0][rows, :] = y.astype(BF16)


def _outproj(x2, merged, wo_all, layer, ln_g, ln_b, tm, with_bf16):
    m = x2.shape[0]
    row = pl.BlockSpec((tm, D_MODEL), lambda i: (i, 0))
    vec = pl.BlockSpec((1, D_MODEL), lambda i: (0, 0))
    return pl.pallas_call(
        _outproj_kernel,
        grid=(m // tm,),
        in_specs=[row, row,
                  pl.BlockSpec((None, D_MODEL, D_MODEL), lambda i: (layer, 0, 0), pipeline_mode=pl.Buffered(1)),
                  vec, vec],
        out_specs=[row, row] if with_bf16 else row,
        out_shape=([jax.ShapeDtypeStruct((m, D_MODEL), F32), jax.ShapeDtypeStruct((m, D_MODEL), BF16)]
                   if with_bf16 else jax.ShapeDtypeStruct((m, D_MODEL), F32)),
        compiler_params=_params("parallel"),
        name="outproj",
    )(x2, merged, wo_all, ln_g, ln_b)


def _rope_table(seq):
    half = ROPE_DIM // 2
    inv_freq = ROPE_THETA ** (-jnp.arange(half, dtype=F32) / half)
    ang = jnp.arange(seq, dtype=F32)[:, None] * inv_freq[None, :]
    cos, sin = jnp.cos(ang), jnp.sin(ang)
    ones = jnp.ones((seq, C_HEAD_DIM - 2 * ROPE_DIM), F32)
    head = jnp.concatenate([cos, cos, -sin, sin, ones], axis=1)
    return jnp.tile(head, (1, LANES // C_HEAD_DIM))


TM_INPROJ = 1024
TM_SGU = 512
TM_COMBINE = 1024
TM_MERGE = 1024
TM_OUTPROJ = 512


def _layer(x2, x_mm, batch, seq, wl, rope_tab, dft_t):
    h_a, misc, g0, g1, g2 = _inproj(x_mm, wl["w_in"], wl["layer"], rope_tab, batch, seq, TM_INPROJ)
    y_a = _fourier(h_a, misc, *dft_t, wl["w_amap"], batch, seq)
    y_b = _sgu(misc, wl["sgu_ln_g"], wl["sgu_ln_b"], wl["sgu_w"], wl["sgu_bias"], TM_SGU)
    groups = (g0.reshape(batch, 1, seq, 3 * CB), g1, g2)
    y_c = _combine([_attn_group(qkv, g) for g, qkv in enumerate(groups)], misc, batch, seq, TM_COMBINE)
    merged = _merge(x_mm, y_a, y_b, y_c, wl["w_in"], wl["w_branch"], wl["layer"], TM_MERGE)
    last = wl["layer"] == DEPTH - 1
    res = _outproj(x2, merged, wl["w_out"], wl["layer"], wl["ln_g"], wl["ln_b"], TM_OUTPROJ, not last)
    return (res, res) if last else tuple(res)


def kernel(x_prompt, x_sample, w_in, w_amap, sgu_ln_g, sgu_ln_b, sgu_w, sgu_b, w_branch, w_out, ln_g, ln_b):
    w_in_bf, w_branch_bf, w_out_bf = w_in.astype(BF16), w_branch.astype(BF16), w_out.astype(BF16)
    layers = []
    for l in range(DEPTH):
        layers.append({
            "layer": l,
            "w_in": w_in_bf,
            "w_amap": w_amap[l].astype(BF16),
            "sgu_ln_g": sgu_ln_g[l][None, :],
            "sgu_ln_b": sgu_ln_b[l][None, :],
            "sgu_w": sgu_w[l].astype(BF16),
            "sgu_bias": jnp.repeat(sgu_b[l].T, GDIM, axis=1),
            "w_branch": w_branch_bf,
            "w_out": w_out_bf,
            "ln_g": ln_g[l][None, :],
            "ln_b": ln_b[l][None, :],
        })
    outs = []
    for x in (x_prompt, x_sample):
        batch, seq, _ = x.shape
        rope_t = _rope_table(seq)
        dft_t = _dft_tables(seq)
        x2 = x_mm = x.reshape(batch * seq, D_MODEL)
        for wl in layers:
            x2, x_mm = _layer(x2, x_mm, batch, seq, wl, rope_t, dft_t)
        outs.append(x2.reshape(batch, seq, D_MODEL))
    return tuple(outs)
```

```python
import functools
import math

import jax
import jax.numpy as jnp
from jax import lax
from jax.experimental import pallas as pl
from jax.experimental.pallas import tpu as pltpu

F32 = jnp.float32
BF16 = jnp.bfloat16

D_MODEL = 2048
DEPTH = 2
W_BR = 512
A_GROUPS = 4
GDIM = 128
B_CHUNK = 128
C_DILATIONS = (1, 4, 16)
C_NG = len(C_DILATIONS)
C_HEADS = 8
C_HEAD_DIM = 64
C_QKV = C_NG * C_HEADS * C_HEAD_DIM
ROPE_DIM = C_HEAD_DIM // 4
ROPE_THETA = 500000.0
N_BRANCH = 3
ALPHA = (2 * DEPTH) ** 0.25
LN_EPS = 1e-5
NEG_BIG = -1e30
HALF_WIN = 64
Q_SCALE = math.log2(math.e) / math.sqrt(C_HEAD_DIM)

LANES = 128
BF16_ROWS = 16
CB = 512
NSLAB = CB // LANES
OFF_A, OFF_AZ, OFF_B, OFF_BZ, OFF_C = 0, 512, 1024, 2048, 2560
OFF_CZ = OFF_C + 3 * C_QKV
OFF_G = OFF_CZ + W_BR
N_MIX_BLOCKS = 15
MISC_AZ, MISC_BZ, MISC_CZ, MISC_U, MISC_V = 0, 1, 2, 3, 4
N_MISC = 5
J_MISC = 1
J_QKV = J_MISC + N_MISC

VMEM_LIMIT = 56 * 1024 * 1024


def _params(*sem):
    return pltpu.CompilerParams(dimension_semantics=sem, vmem_limit_bytes=VMEM_LIMIT)


def _slabs(x):
    return [x[:, c * LANES:(c + 1) * LANES] for c in range(x.shape[1] // LANES)]


def _silu(h):
    return h * jax.nn.sigmoid(h)


def _gelu_tanh(h):
    c = math.sqrt(2.0 / math.pi)
    return h * (0.5 * (1.0 + jnp.tanh(c * (h + 0.044715 * (h * h * h)))))


def _rope_factors(t):
    lane = lax.broadcasted_iota(jnp.int32, t.shape, 1) & (C_HEAD_DIM - 1)
    cosf = jnp.where((lane >= ROPE_DIM) & (lane < 2 * ROPE_DIM), 1.0, t)
    sinf = jnp.where(lane < ROPE_DIM, pltpu.roll(t, LANES - ROPE_DIM, axis=1), 0.0)
    return cosf, sinf, lane < ROPE_DIM // 2


def _rope_slab(hc, cosf, sinf, first_half):
    up = pltpu.roll(hc, LANES - ROPE_DIM // 2, axis=1)
    dn = pltpu.roll(hc, ROPE_DIM // 2, axis=1)
    return hc * cosf + jnp.where(first_half, up, dn) * sinf


ROW_CHUNK = 256


def _inproj_kernel(x_ref, w_ref, tab_ref, ha_ref, misc_ref, g0_ref, g1_ref, g2_ref, tabs_ref, acc_ref,
                   *xb_scratch):
    j = pl.program_id(1)
    tm = x_ref.shape[0]
    lhs_ref = xb_scratch[0] if xb_scratch else x_ref

    @pl.when(j == 0)
    def _():
        if xb_scratch:
            lhs_ref[...] = x_ref[...].astype(BF16)
        tabs_ref[0] = tab_ref[...]
        for g, dil in enumerate(C_DILATIONS[1:], start=1):
            n = tm // dil
            for r in range(dil):
                tabs_ref[g, r * n:(r + 1) * n, :] = tab_ref[pl.ds(r, n, stride=dil), :]

    def chunks(size=ROW_CHUNK):
        for rc in range(tm // size):
            rows = slice(rc * size, (rc + 1) * size)
            yield rc, rows, jnp.dot(lhs_ref[rows, :], w_ref[...], preferred_element_type=F32)

    def fourier_in():
        for _, rows, res in chunks():
            ha_ref[rows, :] = res

    def gate_path():
        for _, rows, res in chunks():
            misc_ref[rows, :] = _silu(res).astype(BF16)

    def gating_in():
        for _, rows, res in chunks():
            misc_ref[rows, :] = _gelu_tanh(res).astype(BF16)

    def qkv(g, part):
        dil, out_ref = C_DILATIONS[g], (g0_ref, g1_ref, g2_ref)[g]
        rotate, is_q = part < 2, part == 0
        size = max(ROW_CHUNK, BF16_ROWS * dil)
        n = size // dil
        for rc, rows, res in chunks(size):
            if dil > 1:
                for c, slab in enumerate(_slabs(res)):
                    acc_ref[rc % 2, c, :size, :] = slab
            for r in range(dil):
                if rotate:
                    factors = _rope_factors(tabs_ref[g, pl.ds(r * (tm // dil) + rc * n, n), :])
                for c in range(NSLAB):
                    cols = slice(c * LANES, (c + 1) * LANES)
                    if dil > 1:
                        hc = acc_ref[rc % 2, c, pl.ds(r, n, stride=dil), :]
                    else:
                        hc = res[:, cols]
                    if rotate:
                        hc = _rope_slab(hc, *factors)
                    if is_q:
                        hc = hc * Q_SCALE
                    if dil > 1:
                        out_ref[r, rc * n:(rc + 1) * n, cols] = hc.astype(BF16)
                    else:
                        out_ref[rows, cols] = hc.astype(BF16)

    leaves = [(0, fourier_in), (J_MISC + MISC_AZ, gate_path), (J_MISC + MISC_U, gating_in)]
    leaves += [(J_QKV + 3 * g + part, functools.partial(qkv, g, part))
               for g in range(C_NG) for part in range(3)]

    def dispatch(sub):
        if len(sub) == 1:
            sub[0][1]()
            return
        mid = len(sub) // 2
        pl.when(j < sub[mid][0])(lambda: dispatch(sub[:mid]))
        pl.when(j >= sub[mid][0])(lambda: dispatch(sub[mid:]))

    dispatch(leaves)


MIX_BLOCK_OF = tuple(off // CB for off in (
    [OFF_A, OFF_AZ, OFF_BZ, OFF_CZ, OFF_B, OFF_B + CB]
    + [OFF_C + part * C_QKV + g * CB for g in range(C_NG) for part in range(3)]))


def _mix_block(j):
    blk = jnp.int32(MIX_BLOCK_OF[0])
    for step in range(1, N_MIX_BLOCKS):
        blk = jnp.where(j == step, MIX_BLOCK_OF[step], blk)
    return blk


def _inproj(x2, w_all, layer, rope_tab, batch, seq, tm):
    m = x2.shape[0]
    nseq = seq // tm

    def part(j, first):
        return jnp.clip(j - first, 0, 2)

    def dil_spec(g):
        dil = C_DILATIONS[g]
        return pl.BlockSpec((None, dil, tm // dil, CB),
                            lambda i, j: (i // nseq, 0, i % nseq, part(j, J_QKV + 3 * g)))

    def dil_shape(g):
        dil = C_DILATIONS[g]
        return jax.ShapeDtypeStruct((batch, dil, seq // dil, 3 * CB), BF16)

    return pl.pallas_call(
        _inproj_kernel,
        grid=(m // tm, N_MIX_BLOCKS),
        in_specs=[
            pl.BlockSpec((tm, D_MODEL), lambda i, j: (i, 0)),
            pl.BlockSpec((None, D_MODEL, CB), lambda i, j: (layer, 0, _mix_block(j))),
            pl.BlockSpec((tm, LANES), lambda i, j: (i % nseq, 0)),
        ],
        out_specs=[
            pl.BlockSpec((tm, CB), lambda i, j: (i, 0)),
            pl.BlockSpec((tm, CB), lambda i, j: (i, jnp.clip(j - J_MISC, 0, N_MISC - 1))),
            pl.BlockSpec((tm, CB), lambda i, j: (i, part(j, J_QKV))),
            dil_spec(1), dil_spec(2),
        ],
        out_shape=[
            jax.ShapeDtypeStruct((m, CB), F32),
            jax.ShapeDtypeStruct((m, N_MISC * CB), BF16),
            jax.ShapeDtypeStruct((m, 3 * CB), BF16),
            dil_shape(1), dil_shape(2),
        ],
        scratch_shapes=[pltpu.VMEM((C_NG, tm, LANES), F32), pltpu.VMEM((2, NSLAB, ROW_CHUNK, LANES), F32)]
        + ([] if x2.dtype == BF16 else [pltpu.VMEM((tm, D_MODEL), BF16)]),
        compiler_params=_params("parallel", "arbitrary"),
        name="inproj",
    )(x2, w_all, rope_tab)


def _sgu_kernel(u_ref, v_ref, z_ref, g_ref, b_ref, ws_ref, bs_ref, o_ref):
    tm = u_ref.shape[0]
    v = v_ref[...].astype(F32)
    mu = jnp.mean(v, axis=-1, keepdims=True)
    vc = v - mu
    var = jnp.mean(vc * vc, axis=-1, keepdims=True)
    vn = (vc * lax.rsqrt(var + LN_EPS) * g_ref[...] + b_ref[...]).astype(BF16)
    bias = bs_ref[...]
    nchunk = tm // B_CHUNK
    mixed = []
    for g in range(A_GROUPS):
        cols = slice(g * GDIM, (g + 1) * GDIM)
        side = jnp.concatenate([vn[n * B_CHUNK:(n + 1) * B_CHUNK, cols] for n in range(nchunk)], axis=1)
        mixed.append(jnp.dot(ws_ref[g], side, preferred_element_type=F32))
    for n in range(nchunk):
        rows = slice(n * B_CHUNK, (n + 1) * B_CHUNK)
        mix = jnp.concatenate([mixed[g][:, n * GDIM:(n + 1) * GDIM] for g in range(A_GROUPS)], axis=1) + bias
        o_ref[rows, :] = (u_ref[rows, :].astype(F32) * mix * z_ref[rows, :].astype(F32)).astype(BF16)


def _sgu(misc, ln_g, ln_b, ws_bf, bias_full, tm):
    m = misc.shape[0]
    col = lambda blk: pl.BlockSpec((tm, CB), lambda i: (i, blk))
    full = lambda shape: pl.BlockSpec(shape, lambda i: (0,) * len(shape))
    return pl.pallas_call(
        _sgu_kernel,
        grid=(m // tm,),
        in_specs=[col(MISC_U), col(MISC_V), col(MISC_BZ), full((1, CB)), full((1, CB)),
                  full((A_GROUPS, B_CHUNK, B_CHUNK)), full((B_CHUNK, CB))],
        out_specs=pl.BlockSpec((tm, CB), lambda i: (i, 0)),
        out_shape=jax.ShapeDtypeStruct((m, CB), BF16),
        compiler_params=_params("parallel"),
        name="sgu",
    )(misc, misc, misc, ln_g, ln_b, ws_bf, bias_full)


SA_ROWS = 8
SA_BLOCKS = 4
SB_ROWS = BF16_ROWS


def _fft_a_kernel(x0_ref, x1_ref, x2_ref, x3_ref, f_ref, y_ref, xin_ref, yout_ref):
    n1 = x0_ref.shape[0]
    for blk in range(SA_BLOCKS):
        srows = slice(blk * SA_ROWS, (blk + 1) * SA_ROWS)
        for g, x_ref in enumerate((x0_ref, x1_ref, x2_ref, x3_ref)):
            xin_ref[blk, g] = x_ref[:, srows, :].reshape(n1 * SA_ROWS, LANES)
        for jj in range(SA_ROWS):
            xj = jnp.concatenate(
                [xin_ref[blk, g, pl.ds(jj, n1, stride=SA_ROWS), :] for g in range(A_GROUPS)],
                axis=1).astype(BF16)
            yj = jnp.dot(f_ref[...], xj, preferred_element_type=F32)
            for g, slab in enumerate(_slabs(yj)):
                yout_ref[blk, g, pl.ds(jj, 2 * n1, stride=SA_ROWS), :] = slab
        for g in range(A_GROUPS):
            y_ref[g, :, srows, :] = yout_ref[blk, g].reshape(2 * n1, SA_ROWS, LANES)


def _fft_b_kernel(yr_ref, yi_ref, m_ref, z_ref, cs_ref, wm_ref, o_ref, f_ref, csw_ref, *, scale):
    n2 = yr_ref.shape[2]
    for g in range(A_GROUPS):
        csw_ref[g] = (jnp.dot(cs_ref[...], wm_ref[g], preferred_element_type=F32) * scale).astype(BF16)

    def body(kk, carry):
        ycat = jnp.concatenate(
            [jnp.concatenate([y_ref[g, kk] for g in range(A_GROUPS)], axis=1) for y_ref in (yr_ref, yi_ref)],
            axis=0).astype(BF16)
        zz = jnp.dot(m_ref[kk], ycat, preferred_element_type=F32).astype(BF16)
        for g in range(A_GROUPS):
            cols = slice(g * GDIM, (g + 1) * GDIM)
            z_ri = jnp.concatenate([zz[:n2, cols], zz[n2:, cols]], axis=1)
            f_ref[g, pl.ds(kk, n2, stride=SB_ROWS), :] = jnp.dot(
                z_ri, csw_ref[g], preferred_element_type=F32)
        return carry

    lax.fori_loop(0, SB_ROWS, body, 0)
    mixed = jnp.concatenate([f_ref[g] for g in range(A_GROUPS)], axis=1)
    gate = z_ref[...].reshape(n2 * SB_ROWS, CB).astype(F32)
    o_ref[...] = (mixed * gate).astype(BF16).reshape(n2, SB_ROWS, CB)


def _fourier(h_a, misc, f1_tab, m_tab, cs_tab, wmap_bf, batch, seq):
    n2 = B_CHUNK
    n1 = seq // n2
    x4 = h_a.reshape(batch, n1, n2, CB)
    step_rows = SA_ROWS * SA_BLOCKS
    xspec = lambda g: pl.BlockSpec((None, n1, step_rows, LANES), lambda b, t: (b, 0, t, g))
    y = pl.pallas_call(
        _fft_a_kernel,
        grid=(batch, n2 // step_rows),
        in_specs=[xspec(0), xspec(1), xspec(2), xspec(3),
                  pl.BlockSpec((2 * n1, n1), lambda b, t: (0, 0))],
        out_specs=pl.BlockSpec((None, A_GROUPS, 2 * n1, step_rows, LANES), lambda b, t: (b, 0, 0, t, 0)),
        out_shape=jax.ShapeDtypeStruct((batch, A_GROUPS, 2 * n1, n2, LANES), F32),
        scratch_shapes=[pltpu.VMEM((SA_BLOCKS, A_GROUPS, n1 * SA_ROWS, LANES), F32),
                        pltpu.VMEM((SA_BLOCKS, A_GROUPS, 2 * n1 * SA_ROWS, LANES), F32)],
        compiler_params=_params("parallel", "parallel"),
        name="fft_a",
    )(x4, x4, x4, x4, f1_tab)
    nk = n1 // SB_ROWS
    z4 = misc.reshape(batch, n2, n1, N_MISC * CB)
    out = pl.pallas_call(
        functools.partial(_fft_b_kernel, scale=1.0 / math.sqrt(seq * GDIM)),
        grid=(nk, batch),
        in_specs=[pl.BlockSpec((None, A_GROUPS, SB_ROWS, n2, LANES), lambda k, b: (b, 0, k, 0, 0)),
                  pl.BlockSpec((None, A_GROUPS, SB_ROWS, n2, LANES), lambda k, b: (b, 0, nk + k, 0, 0)),
                  pl.BlockSpec((SB_ROWS, 2 * n2, 2 * n2), lambda k, b: (k, 0, 0)),
                  pl.BlockSpec((None, n2, SB_ROWS, CB), lambda k, b: (b, 0, k, MISC_AZ)),
                  pl.BlockSpec((2 * GDIM, GDIM), lambda k, b: (0, 0)),
                  pl.BlockSpec((A_GROUPS, GDIM, GDIM), lambda k, b: (0, 0, 0))],
        out_specs=pl.BlockSpec((None, n2, SB_ROWS, CB), lambda k, b: (b, 0, k, 0)),
        out_shape=jax.ShapeDtypeStruct((batch, n2, n1, CB), BF16),
        scratch_shapes=[pltpu.VMEM((A_GROUPS, n2 * SB_ROWS, LANES), F32),
                        pltpu.VMEM((A_GROUPS, 2 * GDIM, GDIM), BF16)],
        compiler_params=_params("parallel", "parallel"),
        name="fft_b",
    )(y, y, m_tab, z4, cs_tab, wmap_bf)
    return out.reshape(batch * seq, CB)


def _dft_tables(seq):
    n2 = B_CHUNK
    n1 = seq // n2
    two_pi = 2.0 * math.pi

    def cs(idx, period):
        ang = (idx % period).astype(F32) * (two_pi / period)
        return jnp.cos(ang), jnp.sin(ang)

    i1 = jnp.arange(n1, dtype=jnp.int32)
    c1, s1 = cs(i1[:, None] * i1[None, :], n1)
    f1_tab = jnp.concatenate([c1, -s1], axis=0).astype(BF16)
    k = i1[:, None, None] + n1 * jnp.arange(n2, dtype=jnp.int32)[None, :, None]
    s2 = jnp.arange(n2, dtype=jnp.int32)[None, None, :]
    cm, sm = cs(k * s2, seq)
    m_tab = jnp.concatenate([jnp.concatenate([cm, sm], axis=2),
                             jnp.concatenate([-sm, cm], axis=2)], axis=1).astype(BF16)
    ic = jnp.arange(GDIM, dtype=jnp.int32)
    cc, sc = cs(ic[:, None] * ic[None, :], GDIM)
    cs_tab = jnp.concatenate([cc, sc], axis=0).astype(BF16)
    return f1_tab, m_tab, cs_tab


Q_SUB = 128
LSE_LANES = LANES // C_HEADS
ATTN_ROWS = 512


def _attn_kernel(q_ref, kc_ref, kp_ref, kn_ref, vc_ref, vp_ref, vn_ref, o_ref, l_ref, *, seq_len, tq):
    t = pl.program_id(2)
    tk = Q_SUB + 2 * HALF_WIN
    ones = jnp.ones((tk, LANES), BF16)
    row = lax.broadcasted_iota(jnp.int32, (2 * Q_SUB, tk), 0) & (Q_SUB - 1)
    col = lax.broadcasted_iota(jnp.int32, (2 * Q_SUB, tk), 1)
    band = jnp.abs(row + HALF_WIN - col) <= HALF_WIN
    lane = lax.broadcasted_iota(jnp.int32, (Q_SUB, LANES), 1)
    lo = lane < C_HEAD_DIM
    head_of_lane = lane // LSE_LANES
    for rr in range(q_ref.shape[0]):
        k = jnp.concatenate([kp_ref[rr], kc_ref[rr], kn_ref[rr]], axis=0)
        v = jnp.concatenate([vp_ref[rr], vc_ref[rr], vn_ref[rr]], axis=0)
        for u in range(tq // Q_SUB):
            rows = slice(u * Q_SUB, (u + 1) * Q_SUB)
            kpos = t * tq + u * Q_SUB - HALF_WIN + col[:1]
            valid = band & ((kpos >= 0) & (kpos < seq_len))
            q = q_ref[rr, rows, :]
            ku, vu = k[u * Q_SUB:u * Q_SUB + tk], v[u * Q_SUB:u * Q_SUB + tk]
            lse_all = jnp.zeros((Q_SUB, LANES), F32)
            for hp in range(C_HEADS // 2):
                cols = slice(hp * LANES, (hp + 1) * LANES)
                qp = q[:, cols]
                zero = jnp.zeros_like(qp)
                q2 = jnp.concatenate([jnp.where(lo, qp, zero), jnp.where(lo, zero, qp)], axis=0)
                s = lax.dot_general(q2, ku[:, cols], (((1,), (1,)), ((), ())), preferred_element_type=F32)
                s = jnp.where(valid, s, NEG_BIG)
                m = jnp.max(s, axis=1, keepdims=True)
                p = jnp.exp2(s - m).astype(BF16)
                pv = jnp.dot(p, jnp.concatenate([vu[:, cols], ones], axis=1), preferred_element_type=F32)
                den = pv[:, LANES:]
                o = pv[:, :LANES] / den
                lse = m * math.log(2.0) + jnp.log(den)
                o_ref[rr, rows, cols] = jnp.where(lo, o[:Q_SUB], o[Q_SUB:]).astype(BF16)
                lse_all = jnp.where(head_of_lane == 2 * hp, lse[:Q_SUB], lse_all)
                lse_all = jnp.where(head_of_lane == 2 * hp + 1, lse[Q_SUB:], lse_all)
            l_ref[rr, rows, :] = lse_all


def _attn_group(qkv, g):
    batch, dil, ln, _ = qkv.shape
    tq = min(ATTN_ROWS, ln)
    nres = min(dil, ATTN_ROWS // tq)
    per = tq // HALF_WIN
    nhalo = ln // HALF_WIN

    def cur(blk):
        return pl.BlockSpec((None, nres, tq, CB), lambda b, r, t: (b, r, t, blk))

    def before(blk):
        return pl.BlockSpec((None, nres, HALF_WIN, CB),
                            lambda b, r, t: (b, r, jnp.maximum(t * per - 1, 0), blk))

    def after(blk):
        return pl.BlockSpec((None, nres, HALF_WIN, CB),
                            lambda b, r, t: (b, r, jnp.minimum((t + 1) * per, nhalo - 1), blk))

    tile = pl.BlockSpec((None, nres, tq, CB), lambda b, r, t: (b, r, t, 0))
    ltile = pl.BlockSpec((None, nres, tq, LANES), lambda b, r, t: (b, r, t, 0))
    return pl.pallas_call(
        functools.partial(_attn_kernel, seq_len=ln, tq=tq),
        grid=(batch, dil // nres, ln // tq),
        in_specs=[cur(0), cur(1), before(1), after(1), cur(2), before(2), after(2)],
        out_specs=[tile, ltile],
        out_shape=[jax.ShapeDtypeStruct((batch, dil, ln, CB), BF16),
                   jax.ShapeDtypeStruct((batch, dil, ln, LANES), F32)],
        compiler_params=_params("parallel", "parallel", "parallel"),
        name=f"attn{g}",
    )(*([qkv] * 7))


def _combine_kernel(o0_ref, l0_ref, o1_ref, l1_ref, o2_ref, l2_ref, cz_ref, y_ref, os1_ref, os2_ref, ls_ref):
    tm = o0_ref.shape[0]
    for gi, (o_ref, l_ref, os_ref) in enumerate(((o1_ref, l1_ref, os1_ref), (o2_ref, l2_ref, os2_ref))):
        dil = C_DILATIONS[gi + 1]
        n = tm // dil
        for r in range(dil):
            rows = pl.ds(r, n, stride=dil)
            ls_ref[gi, rows, :] = l_ref[r]
            for c in range(NSLAB):
                os_ref[c, rows, :] = o_ref[r, :, c * LANES:(c + 1) * LANES].astype(F32)
    l0, l1, l2 = l0_ref[...], ls_ref[0], ls_ref[1]
    mx = jnp.maximum(jnp.maximum(l0, l1), l2)
    es = [jnp.exp(l - mx) for l in (l0, l1, l2)]
    den = es[0] + es[1] + es[2]
    spread = (lax.broadcasted_iota(jnp.int32, (LANES, CB), 0)
              == (lax.broadcasted_iota(jnp.int32, (LANES, CB), 1) // C_HEAD_DIM) * LSE_LANES).astype(BF16)
    wide = []
    for e in es:
        w = e / den
        hi = w.astype(BF16)
        lo = (w - hi.astype(F32)).astype(BF16)
        wide.append(jnp.dot(hi, spread, preferred_element_type=F32) + jnp.dot(lo, spread, preferred_element_type=F32))
    for c in range(NSLAB):
        cols = slice(c * LANES, (c + 1) * LANES)
        mix = (wide[0][:, cols] * o0_ref[:, cols].astype(F32) + wide[1][:, cols] * os1_ref[c]
               + wide[2][:, cols] * os2_ref[c])
        y_ref[:, cols] = (mix * cz_ref[:, cols].astype(F32)).astype(BF16)


def _combine(res, misc, batch, seq, tm):
    (o0, l0), (o1, l1), (o2, l2) = res
    nseq = seq // tm
    def spec(dil, width):
        if dil == 1:
            return pl.BlockSpec((None, None, tm, width), lambda i: (i // nseq, 0, i % nseq, 0))
        return pl.BlockSpec((None, dil, tm // dil, width), lambda i: (i // nseq, 0, i % nseq, 0))

    in_specs = [spec(dil, width) for dil in C_DILATIONS for width in (CB, LANES)]
    return pl.pallas_call(
        _combine_kernel,
        grid=(batch * nseq,),
        in_specs=in_specs + [pl.BlockSpec((tm, CB), lambda i: (i, MISC_CZ))],
        out_specs=pl.BlockSpec((tm, CB), lambda i: (i, 0)),
        out_shape=jax.ShapeDtypeStruct((batch * seq, CB), BF16),
        scratch_shapes=[pltpu.VMEM((NSLAB, tm, LANES), F32), pltpu.VMEM((NSLAB, tm, LANES), F32),
                        pltpu.VMEM((C_NG - 1, tm, LANES), F32)],
        compiler_params=_params("parallel"),
        name="combine",
    )(o0, l0, o1, l1, o2, l2, misc)


def _merge_kernel(x_ref, ya_ref, yb_ref, yc_ref, wg0_ref, wg1_ref, wg2_ref, wb_ref, o_ref, *xb_scratch):
    if xb_scratch:
        @pl.when(pl.program_id(1) == 0)
        def _():
            xb_scratch[0][...] = x_ref[...].astype(BF16)

    xb = xb_scratch[0][...] if xb_scratch else x_ref[...]
    acc = None
    for kk, (y_ref, wg_ref) in enumerate(((ya_ref, wg0_ref), (yb_ref, wg1_ref), (yc_ref, wg2_ref))):
        gate = jax.nn.sigmoid(jnp.dot(xb, wg_ref[...], preferred_element_type=F32))
        term = gate * jnp.dot(y_ref[...], wb_ref[kk], preferred_element_type=F32)
        acc = term if acc is None else acc + term
    o_ref[...] = acc.astype(BF16)


def _merge(x2, ya, yb, yc, w_all, wbr_all, layer, tm):
    m = x2.shape[0]
    ncol = D_MODEL // CB
    ybs = pl.BlockSpec((tm, CB), lambda i, c: (i, 0))

    def gate_spec(kk):
        return pl.BlockSpec((None, D_MODEL, CB), lambda i, c: (layer, 0, OFF_G // CB + kk * ncol + c))

    return pl.pallas_call(
        _merge_kernel,
        grid=(m // tm, ncol),
        in_specs=[pl.BlockSpec((tm, D_MODEL), lambda i, c: (i, 0)), ybs, ybs, ybs,
                  gate_spec(0), gate_spec(1), gate_spec(2),
                  pl.BlockSpec((None, N_BRANCH, W_BR, CB), lambda i, c: (layer, 0, 0, c))],
        out_specs=pl.BlockSpec((tm, CB), lambda i, c: (i, c)),
        out_shape=jax.ShapeDtypeStruct((m, D_MODEL), BF16),
        scratch_shapes=[] if x2.dtype == BF16 else [pltpu.VMEM((tm, D_MODEL), BF16)],
        compiler_params=_params("parallel", "arbitrary"),
        name="merge",
    )(x2, ya, yb, yc, w_all, w_all, w_all, wbr_all)


def _outproj_kernel(x_ref, m_ref, wo_ref, g_ref, b_ref, o_ref, *ob_ref):
    for rc in range(x_ref.shape[0] // ROW_CHUNK):
        rows = slice(rc * ROW_CHUNK, (rc + 1) * ROW_CHUNK)
        out = jnp.dot(m_ref[rows, :], wo_ref[...], preferred_element_type=F32)
        z = ALPHA * x_ref[rows, :] + out
        mu = jnp.mean(z, axis=-1, keepdims=True)
        zc = z - mu
        var = jnp.mean(zc * zc, axis=-1, keepdims=True)
        y = zc * lax.rsqrt(var + LN_EPS) * g_ref[...] + b_ref[...]
        o_ref[rows, :] = y
        if ob_ref:
            ob_ref[0][rows, :] = y.astype(BF16)


def _outproj(x2, merged, wo_all, layer, ln_g, ln_b, tm, with_bf16):
    m = x2.shape[0]
    row = pl.BlockSpec((tm, D_MODEL), lambda i: (i, 0))
    vec = pl.BlockSpec((1, D_MODEL), lambda i: (0, 0))
    return pl.pallas_call(
        _outproj_kernel,
        grid=(m // tm,),
        in_specs=[row, row,
                  pl.BlockSpec((None, D_MODEL, D_MODEL), lambda i: (layer, 0, 0), pipeline_mode=pl.Buffered(1)),
                  vec, vec],
        out_specs=[row, row] if with_bf16 else row,
        out_shape=([jax.ShapeDtypeStruct((m, D_MODEL), F32), jax.ShapeDtypeStruct((m, D_MODEL), BF16)]
                   if with_bf16 else jax.ShapeDtypeStruct((m, D_MODEL), F32)),
        compiler_params=_params("parallel"),
        name="outproj",
    )(x2, merged, wo_all, ln_g, ln_b)


def _rope_table(seq):
    half = ROPE_DIM // 2
    inv_freq = ROPE_THETA ** (-jnp.arange(half, dtype=F32) / half)
    ang = jnp.arange(seq, dtype=F32)[:, None] * inv_freq[None, :]
    cos, sin = jnp.cos(ang), jnp.sin(ang)
    ones = jnp.ones((seq, C_HEAD_DIM - 2 * ROPE_DIM), F32)
    head = jnp.concatenate([cos, cos, -sin, sin, ones], axis=1)
    return jnp.tile(head, (1, LANES // C_HEAD_DIM))


TM_INPROJ = 1024
TM_INPROJ_BF16 = 2048
TM_SGU = 512
TM_COMBINE = 1024
TM_MERGE = 1024
TM_OUTPROJ = 512


def _layer(x2, x_mm, batch, seq, wl, rope_tab, dft_t):
    tm_in = TM_INPROJ_BF16 if x_mm.dtype == BF16 else TM_INPROJ
    h_a, misc, g0, g1, g2 = _inproj(x_mm, wl["w_in"], wl["layer"], rope_tab, batch, seq, tm_in)
    y_a = _fourier(h_a, misc, *dft_t, wl["w_amap"], batch, seq)
    y_b = _sgu(misc, wl["sgu_ln_g"], wl["sgu_ln_b"], wl["sgu_w"], wl["sgu_bias"], TM_SGU)
    groups = (g0.reshape(batch, 1, seq, 3 * CB), g1, g2)
    y_c = _combine([_attn_group(qkv, g) for g, qkv in enumerate(groups)], misc, batch, seq, TM_COMBINE)
    merged = _merge(x_mm, y_a, y_b, y_c, wl["w_in"], wl["w_branch"], wl["layer"], TM_MERGE)
    last = wl["layer"] == DEPTH - 1
    res = _outproj(x2, merged, wl["w_out"], wl["layer"], wl["ln_g"], wl["ln_b"], TM_OUTPROJ, not last)
    return (res, res) if last else tuple(res)


def kernel(x_prompt, x_sample, w_in, w_amap, sgu_ln_g, sgu_ln_b, sgu_w, sgu_b, w_branch, w_out, ln_g, ln_b):
    w_in_bf, w_branch_bf, w_out_bf = w_in.astype(BF16), w_branch.astype(BF16), w_out.astype(BF16)
    layers = []
    for l in range(DEPTH):
        layers.append({
            "layer": l,
            "w_in": w_in_bf,
            "w_amap": w_amap[l].astype(BF16),
            "sgu_ln_g": sgu_ln_g[l][None, :],
            "sgu_ln_b": sgu_ln_b[l][None, :],
            "sgu_w": sgu_w[l].astype(BF16),
            "sgu_bias": jnp.repeat(sgu_b[l].T, GDIM, axis=1),
            "w_branch": w_branch_bf,
            "w_out": w_out_bf,
            "ln_g": ln_g[l][None, :],
            "ln_b": ln_b[l][None, :],
        })
    outs = []
    for x in (x_prompt, x_sample):
        batch, seq, _ = x.shape
        rope_t = _rope_table(seq)
        dft_t = _dft_tables(seq)
        x2 = x_mm = x.reshape(batch * seq, D_MODEL)
        for wl in layers:
            x2, x_mm = _layer(x2, x_mm, batch, seq, wl, rope_t, dft_t)
        outs.append(x2.reshape(batch, seq, D_MODEL))
    return tuple(outs)
```

```python
import functools
import math

import jax
import jax.numpy as jnp
from jax import lax
from jax.experimental import pallas as pl
from jax.experimental.pallas import tpu as pltpu

F32 = jnp.float32
BF16 = jnp.bfloat16

D_MODEL = 2048
DEPTH = 2
W_BR = 512
A_GROUPS = 4
GDIM = 128
B_CHUNK = 128
C_DILATIONS = (1, 4, 16)
C_NG = len(C_DILATIONS)
C_HEADS = 8
C_HEAD_DIM = 64
C_QKV = C_NG * C_HEADS * C_HEAD_DIM
ROPE_DIM = C_HEAD_DIM // 4
ROPE_THETA = 500000.0
N_BRANCH = 3
ALPHA = (2 * DEPTH) ** 0.25
LN_EPS = 1e-5
NEG_BIG = -1e30
HALF_WIN = 64
Q_SCALE = math.log2(math.e) / math.sqrt(C_HEAD_DIM)

LANES = 128
BF16_ROWS = 16
CB = 512
NSLAB = CB // LANES
OFF_A, OFF_AZ, OFF_B, OFF_BZ, OFF_C = 0, 512, 1024, 2048, 2560
OFF_CZ = OFF_C + 3 * C_QKV
OFF_G = OFF_CZ + W_BR
N_MIX_BLOCKS = 15
MISC_AZ, MISC_BZ, MISC_CZ, MISC_U, MISC_V = 0, 1, 2, 3, 4
N_MISC = 5
J_MISC = 1
J_QKV = J_MISC + N_MISC

VMEM_LIMIT = 56 * 1024 * 1024


def _params(*sem):
    return pltpu.CompilerParams(dimension_semantics=sem, vmem_limit_bytes=VMEM_LIMIT)


def _slabs(x):
    return [x[:, c * LANES:(c + 1) * LANES] for c in range(x.shape[1] // LANES)]


def _silu(h):
    return h * jax.nn.sigmoid(h)


def _gelu_tanh(h):
    c = math.sqrt(2.0 / math.pi)
    return h * (0.5 * (1.0 + jnp.tanh(c * (h + 0.044715 * (h * h * h)))))


def _rope_factors(t):
    lane = lax.broadcasted_iota(jnp.int32, t.shape, 1) & (C_HEAD_DIM - 1)
    cosf = jnp.where((lane >= ROPE_DIM) & (lane < 2 * ROPE_DIM), 1.0, t)
    sinf = jnp.where(lane < ROPE_DIM, pltpu.roll(t, LANES - ROPE_DIM, axis=1), 0.0)
    return cosf, sinf, lane < ROPE_DIM // 2


def _rope_slab(hc, cosf, sinf, first_half):
    up = pltpu.roll(hc, LANES - ROPE_DIM // 2, axis=1)
    dn = pltpu.roll(hc, ROPE_DIM // 2, axis=1)
    return hc * cosf + jnp.where(first_half, up, dn) * sinf


ROW_CHUNK = 256


def _inproj_kernel(x_ref, w_ref, tab_ref, ha_ref, misc_ref, g0_ref, g1_ref, g2_ref, tabs_ref, acc_ref,
                   *xb_scratch):
    j = pl.program_id(1)
    tm = x_ref.shape[0]
    lhs_ref = xb_scratch[0] if xb_scratch else x_ref

    @pl.when(j == 0)
    def _():
        if xb_scratch:
            lhs_ref[...] = x_ref[...].astype(BF16)
        tabs_ref[0] = tab_ref[...]
        for g, dil in enumerate(C_DILATIONS[1:], start=1):
            n = tm // dil
            for r in range(dil):
                tabs_ref[g, r * n:(r + 1) * n, :] = tab_ref[pl.ds(r, n, stride=dil), :]

    def chunks(size=ROW_CHUNK):
        for rc in range(tm // size):
            rows = slice(rc * size, (rc + 1) * size)
            yield rc, rows, jnp.dot(lhs_ref[rows, :], w_ref[...], preferred_element_type=F32)

    def fourier_in():
        for _, rows, res in chunks():
            ha_ref[rows, :] = res

    def gate_path():
        for _, rows, res in chunks():
            misc_ref[rows, :] = _silu(res).astype(BF16)

    def gating_in():
        for _, rows, res in chunks():
            misc_ref[rows, :] = _gelu_tanh(res).astype(BF16)

    def qkv(g):
        dil, out_ref = C_DILATIONS[g], (g0_ref, g1_ref, g2_ref)[g]
        part = j - (J_QKV + 3 * g)
        is_v = part == 2
        scale = jnp.where(part == 0, Q_SCALE, 1.0).astype(F32)
        size = max(ROW_CHUNK, BF16_ROWS * dil)
        n = size // dil
        for rc, rows, res in chunks(size):
            if dil > 1:
                for c, slab in enumerate(_slabs(res)):
                    acc_ref[rc % 2, c, :size, :] = slab
            for r in range(dil):
                cosf, sinf, first_half = _rope_factors(tabs_ref[g, pl.ds(r * (tm // dil) + rc * n, n), :])
                cosf = jnp.where(is_v, 1.0, cosf) * scale
                sinf = jnp.where(is_v, 0.0, sinf) * scale
                for c in range(NSLAB):
                    cols = slice(c * LANES, (c + 1) * LANES)
                    if dil > 1:
                        hc = acc_ref[rc % 2, c, pl.ds(r, n, stride=dil), :]
                    else:
                        hc = res[:, cols]
                    hc = _rope_slab(hc, cosf, sinf, first_half)
                    if dil > 1:
                        out_ref[r, rc * n:(rc + 1) * n, cols] = hc.astype(BF16)
                    else:
                        out_ref[rows, cols] = hc.astype(BF16)

    leaves = [(0, fourier_in), (J_MISC + MISC_AZ, gate_path), (J_MISC + MISC_U, gating_in)]
    leaves += [(J_QKV + 3 * g, functools.partial(qkv, g)) for g in range(C_NG)]

    def dispatch(sub):
        if len(sub) == 1:
            sub[0][1]()
            return
        mid = len(sub) // 2
        pl.when(j < sub[mid][0])(lambda: dispatch(sub[:mid]))
        pl.when(j >= sub[mid][0])(lambda: dispatch(sub[mid:]))

    dispatch(leaves)


MIX_BLOCK_OF = tuple(off // CB for off in (
    [OFF_A, OFF_AZ, OFF_BZ, OFF_CZ, OFF_B, OFF_B + CB]
    + [OFF_C + part * C_QKV + g * CB for g in range(C_NG) for part in range(3)]))


def _mix_block(j):
    blk = jnp.int32(MIX_BLOCK_OF[0])
    for step in range(1, N_MIX_BLOCKS):
        blk = jnp.where(j == step, MIX_BLOCK_OF[step], blk)
    return blk


def _inproj(x2, w_all, layer, rope_tab, batch, seq, tm):
    m = x2.shape[0]
    nseq = seq // tm

    def part(j, first):
        return jnp.clip(j - first, 0, 2)

    def dil_spec(g):
        dil = C_DILATIONS[g]
        return pl.BlockSpec((None, dil, tm // dil, CB),
                            lambda i, j: (i // nseq, 0, i % nseq, part(j, J_QKV + 3 * g)))

    def dil_shape(g):
        dil = C_DILATIONS[g]
        return jax.ShapeDtypeStruct((batch, dil, seq // dil, 3 * CB), BF16)

    return pl.pallas_call(
        _inproj_kernel,
        grid=(m // tm, N_MIX_BLOCKS),
        in_specs=[
            pl.BlockSpec((tm, D_MODEL), lambda i, j: (i, 0)),
            pl.BlockSpec((None, D_MODEL, CB), lambda i, j: (layer, 0, _mix_block(j))),
            pl.BlockSpec((tm, LANES), lambda i, j: (i % nseq, 0)),
        ],
        out_specs=[
            pl.BlockSpec((tm, CB), lambda i, j: (i, 0)),
            pl.BlockSpec((tm, CB), lambda i, j: (i, jnp.clip(j - J_MISC, 0, N_MISC - 1))),
            pl.BlockSpec((tm, CB), lambda i, j: (i, part(j, J_QKV))),
            dil_spec(1), dil_spec(2),
        ],
        out_shape=[
            jax.ShapeDtypeStruct((m, CB), F32),
            jax.ShapeDtypeStruct((m, N_MISC * CB), BF16),
            jax.ShapeDtypeStruct((m, 3 * CB), BF16),
            dil_shape(1), dil_shape(2),
        ],
        scratch_shapes=[pltpu.VMEM((C_NG, tm, LANES), F32), pltpu.VMEM((2, NSLAB, ROW_CHUNK, LANES), F32)]
        + ([] if x2.dtype == BF16 else [pltpu.VMEM((tm, D_MODEL), BF16)]),
        compiler_params=_params("parallel", "arbitrary"),
        name="inproj",
    )(x2, w_all, rope_tab)


def _sgu_kernel(u_ref, v_ref, z_ref, g_ref, b_ref, ws_ref, bs_ref, o_ref):
    tm = u_ref.shape[0]
    v = v_ref[...].astype(F32)
    mu = jnp.mean(v, axis=-1, keepdims=True)
    vc = v - mu
    var = jnp.mean(vc * vc, axis=-1, keepdims=True)
    vn = (vc * lax.rsqrt(var + LN_EPS) * g_ref[...] + b_ref[...]).astype(BF16)
    bias = bs_ref[...]
    nchunk = tm // B_CHUNK
    mixed = []
    for g in range(A_GROUPS):
        cols = slice(g * GDIM, (g + 1) * GDIM)
        side = jnp.concatenate([vn[n * B_CHUNK:(n + 1) * B_CHUNK, cols] for n in range(nchunk)], axis=1)
        mixed.append(jnp.dot(ws_ref[g], side, preferred_element_type=F32))
    for n in range(nchunk):
        rows = slice(n * B_CHUNK, (n + 1) * B_CHUNK)
        mix = jnp.concatenate([mixed[g][:, n * GDIM:(n + 1) * GDIM] for g in range(A_GROUPS)], axis=1) + bias
        o_ref[rows, :] = (u_ref[rows, :].astype(F32) * mix * z_ref[rows, :].astype(F32)).astype(BF16)


def _sgu(misc, ln_g, ln_b, ws_bf, bias_full, tm):
    m = misc.shape[0]
    col = lambda blk: pl.BlockSpec((tm, CB), lambda i: (i, blk))
    full = lambda shape: pl.BlockSpec(shape, lambda i: (0,) * len(shape))
    return pl.pallas_call(
        _sgu_kernel,
        grid=(m // tm,),
        in_specs=[col(MISC_U), col(MISC_V), col(MISC_BZ), full((1, CB)), full((1, CB)),
                  full((A_GROUPS, B_CHUNK, B_CHUNK)), full((B_CHUNK, CB))],
        out_specs=pl.BlockSpec((tm, CB), lambda i: (i, 0)),
        out_shape=jax.ShapeDtypeStruct((m, CB), BF16),
        compiler_params=_params("parallel"),
        name="sgu",
    )(misc, misc, misc, ln_g, ln_b, ws_bf, bias_full)


SA_ROWS = 8
SA_BLOCKS = 4
SB_ROWS = BF16_ROWS


def _fft_a_kernel(x0_ref, x1_ref, x2_ref, x3_ref, f_ref, y_ref, xin_ref, yout_ref):
    n1 = x0_ref.shape[0]
    for blk in range(SA_BLOCKS):
        srows = slice(blk * SA_ROWS, (blk + 1) * SA_ROWS)
        for g, x_ref in enumerate((x0_ref, x1_ref, x2_ref, x3_ref)):
            xin_ref[blk, g] = x_ref[:, srows, :].reshape(n1 * SA_ROWS, LANES)
        for jj in range(SA_ROWS):
            xj = jnp.concatenate(
                [xin_ref[blk, g, pl.ds(jj, n1, stride=SA_ROWS), :] for g in range(A_GROUPS)],
                axis=1).astype(BF16)
            yj = jnp.dot(f_ref[...], xj, preferred_element_type=F32)
            for g, slab in enumerate(_slabs(yj)):
                yout_ref[blk, g, pl.ds(jj, 2 * n1, stride=SA_ROWS), :] = slab
        for g in range(A_GROUPS):
            y_ref[g, :, srows, :] = yout_ref[blk, g].reshape(2 * n1, SA_ROWS, LANES)


def _fft_b_kernel(yr_ref, yi_ref, m_ref, z_ref, cs_ref, wm_ref, o_ref, f_ref, csw_ref, *, scale):
    n2 = yr_ref.shape[2]
    for g in range(A_GROUPS):
        csw_ref[g] = (jnp.dot(cs_ref[...], wm_ref[g], preferred_element_type=F32) * scale).astype(BF16)

    def body(kk, carry):
        ycat = jnp.concatenate(
            [jnp.concatenate([y_ref[g, kk] for g in range(A_GROUPS)], axis=1) for y_ref in (yr_ref, yi_ref)],
            axis=0).astype(BF16)
        zz = jnp.dot(m_ref[kk], ycat, preferred_element_type=F32).astype(BF16)
        for g in range(A_GROUPS):
            cols = slice(g * GDIM, (g + 1) * GDIM)
            z_ri = jnp.concatenate([zz[:n2, cols], zz[n2:, cols]], axis=1)
            f_ref[g, pl.ds(kk, n2, stride=SB_ROWS), :] = jnp.dot(
                z_ri, csw_ref[g], preferred_element_type=F32)
        return carry

    lax.fori_loop(0, SB_ROWS, body, 0)
    mixed = jnp.concatenate([f_ref[g] for g in range(A_GROUPS)], axis=1)
    gate = z_ref[...].reshape(n2 * SB_ROWS, CB).astype(F32)
    o_ref[...] = (mixed * gate).astype(BF16).reshape(n2, SB_ROWS, CB)


def _fourier(h_a, misc, f1_tab, m_tab, cs_tab, wmap_bf, batch, seq):
    n2 = B_CHUNK
    n1 = seq // n2
    x4 = h_a.reshape(batch, n1, n2, CB)
    step_rows = SA_ROWS * SA_BLOCKS
    xspec = lambda g: pl.BlockSpec((None, n1, step_rows, LANES), lambda b, t: (b, 0, t, g))
    y = pl.pallas_call(
        _fft_a_kernel,
        grid=(batch, n2 // step_rows),
        in_specs=[xspec(0), xspec(1), xspec(2), xspec(3),
                  pl.BlockSpec((2 * n1, n1), lambda b, t: (0, 0))],
        out_specs=pl.BlockSpec((None, A_GROUPS, 2 * n1, step_rows, LANES), lambda b, t: (b, 0, 0, t, 0)),
        out_shape=jax.ShapeDtypeStruct((batch, A_GROUPS, 2 * n1, n2, LANES), F32),
        scratch_shapes=[pltpu.VMEM((SA_BLOCKS, A_GROUPS, n1 * SA_ROWS, LANES), F32),
                        pltpu.VMEM((SA_BLOCKS, A_GROUPS, 2 * n1 * SA_ROWS, LANES), F32)],
        compiler_params=_params("parallel", "parallel"),
        name="fft_a",
    )(x4, x4, x4, x4, f1_tab)
    nk = n1 // SB_ROWS
    z4 = misc.reshape(batch, n2, n1, N_MISC * CB)
    out = pl.pallas_call(
        functools.partial(_fft_b_kernel, scale=1.0 / math.sqrt(seq * GDIM)),
        grid=(nk, batch),
        in_specs=[pl.BlockSpec((None, A_GROUPS, SB_ROWS, n2, LANES), lambda k, b: (b, 0, k, 0, 0)),
                  pl.BlockSpec((None, A_GROUPS, SB_ROWS, n2, LANES), lambda k, b: (b, 0, nk + k, 0, 0)),
                  pl.BlockSpec((SB_ROWS, 2 * n2, 2 * n2), lambda k, b: (k, 0, 0)),
                  pl.BlockSpec((None, n2, SB_ROWS, CB), lambda k, b: (b, 0, k, MISC_AZ)),
                  pl.BlockSpec((2 * GDIM, GDIM), lambda k, b: (0, 0)),
                  pl.BlockSpec((A_GROUPS, GDIM, GDIM), lambda k, b: (0, 0, 0))],
        out_specs=pl.BlockSpec((None, n2, SB_ROWS, CB), lambda k, b: (b, 0, k, 0)),
        out_shape=jax.ShapeDtypeStruct((batch, n2, n1, CB), BF16),
        scratch_shapes=[pltpu.VMEM((A_GROUPS, n2 * SB_ROWS, LANES), F32),
                        pltpu.VMEM((A_GROUPS, 2 * GDIM, GDIM), BF16)],
        compiler_params=_params("parallel", "parallel"),
        name="fft_b",
    )(y, y, m_tab, z4, cs_tab, wmap_bf)
    return out.reshape(batch * seq, CB)


def _dft_tables(seq):
    n2 = B_CHUNK
    n1 = seq // n2
    two_pi = 2.0 * math.pi

    def cs(idx, period):
        ang = (idx % period).astype(F32) * (two_pi / period)
        return jnp.cos(ang), jnp.sin(ang)

    i1 = jnp.arange(n1, dtype=jnp.int32)
    c1, s1 = cs(i1[:, None] * i1[None, :], n1)
    f1_tab = jnp.concatenate([c1, -s1], axis=0).astype(BF16)
    k = i1[:, None, None] + n1 * jnp.arange(n2, dtype=jnp.int32)[None, :, None]
    s2 = jnp.arange(n2, dtype=jnp.int32)[None, None, :]
    cm, sm = cs(k * s2, seq)
    m_tab = jnp.concatenate([jnp.concatenate([cm, sm], axis=2),
                             jnp.concatenate([-sm, cm], axis=2)], axis=1).astype(BF16)
    ic = jnp.arange(GDIM, dtype=jnp.int32)
    cc, sc = cs(ic[:, None] * ic[None, :], GDIM)
    cs_tab = jnp.concatenate([cc, sc], axis=0).astype(BF16)
    return f1_tab, m_tab, cs_tab


Q_SUB = 128
LSE_LANES = LANES // C_HEADS
ATTN_ROWS = 512


def _attn_kernel(q_ref, kc_ref, kp_ref, kn_ref, vc_ref, vp_ref, vn_ref, o_ref, l_ref, *, seq_len, tq):
    t = pl.program_id(2)
    tk = Q_SUB + 2 * HALF_WIN
    ones = jnp.ones((tk, LANES), BF16)
    row = lax.broadcasted_iota(jnp.int32, (2 * Q_SUB, tk), 0) & (Q_SUB - 1)
    col = lax.broadcasted_iota(jnp.int32, (2 * Q_SUB, tk), 1)
    band = jnp.abs(row + HALF_WIN - col) <= HALF_WIN
    lane = lax.broadcasted_iota(jnp.int32, (Q_SUB, LANES), 1)
    lo = lane < C_HEAD_DIM
    head_of_lane = lane // LSE_LANES
    for rr in range(q_ref.shape[0]):
        k = jnp.concatenate([kp_ref[rr], kc_ref[rr], kn_ref[rr]], axis=0)
        v = jnp.concatenate([vp_ref[rr], vc_ref[rr], vn_ref[rr]], axis=0)
        for u in range(tq // Q_SUB):
            rows = slice(u * Q_SUB, (u + 1) * Q_SUB)
            kpos = t * tq + u * Q_SUB - HALF_WIN + col[:1]
            valid = band & ((kpos >= 0) & (kpos < seq_len))
            q = q_ref[rr, rows, :]
            ku, vu = k[u * Q_SUB:u * Q_SUB + tk], v[u * Q_SUB:u * Q_SUB + tk]
            lse_all = jnp.zeros((Q_SUB, LANES), F32)
            for hp in range(C_HEADS // 2):
                cols = slice(hp * LANES, (hp + 1) * LANES)
                qp = q[:, cols]
                zero = jnp.zeros_like(qp)
                q2 = jnp.concatenate([jnp.where(lo, qp, zero), jnp.where(lo, zero, qp)], axis=0)
                s = lax.dot_general(q2, ku[:, cols], (((1,), (1,)), ((), ())), preferred_element_type=F32)
                s = jnp.where(valid, s, NEG_BIG)
                m = jnp.max(s, axis=1, keepdims=True)
                p = jnp.exp2(s - m).astype(BF16)
                pv = jnp.dot(p, jnp.concatenate([vu[:, cols], ones], axis=1), preferred_element_type=F32)
                den = pv[:, LANES:]
                o = pv[:, :LANES] / den
                lse = m * math.log(2.0) + jnp.log(den)
                o_ref[rr, rows, cols] = jnp.where(lo, o[:Q_SUB], o[Q_SUB:]).astype(BF16)
                lse_all = jnp.where(head_of_lane == 2 * hp, lse[:Q_SUB], lse_all)
                lse_all = jnp.where(head_of_lane == 2 * hp + 1, lse[Q_SUB:], lse_all)
            l_ref[rr, rows, :] = lse_all


def _attn_group(qkv, g):
    batch, dil, ln, _ = qkv.shape
    tq = min(ATTN_ROWS, ln)
    nres = min(dil, ATTN_ROWS // tq)
    per = tq // HALF_WIN
    nhalo = ln // HALF_WIN

    def cur(blk):
        return pl.BlockSpec((None, nres, tq, CB), lambda b, r, t: (b, r, t, blk))

    def before(blk):
        return pl.BlockSpec((None, nres, HALF_WIN, CB),
                            lambda b, r, t: (b, r, jnp.maximum(t * per - 1, 0), blk))

    def after(blk):
        return pl.BlockSpec((None, nres, HALF_WIN, CB),
                            lambda b, r, t: (b, r, jnp.minimum((t + 1) * per, nhalo - 1), blk))

    tile = pl.BlockSpec((None, nres, tq, CB), lambda b, r, t: (b, r, t, 0))
    ltile = pl.BlockSpec((None, nres, tq, LANES), lambda b, r, t: (b, r, t, 0))
    return pl.pallas_call(
        functools.partial(_attn_kernel, seq_len=ln, tq=tq),
        grid=(batch, dil // nres, ln // tq),
        in_specs=[cur(0), cur(1), before(1), after(1), cur(2), before(2), after(2)],
        out_specs=[tile, ltile],
        out_shape=[jax.ShapeDtypeStruct((batch, dil, ln, CB), BF16),
                   jax.ShapeDtypeStruct((batch, dil, ln, LANES), F32)],
        compiler_params=_params("parallel", "parallel", "parallel"),
        name=f"attn{g}",
    )(*([qkv] * 7))


def _combine_kernel(o0_ref, l0_ref, o1_ref, l1_ref, o2_ref, l2_ref, cz_ref, y_ref, os1_ref, os2_ref, ls_ref):
    tm = o0_ref.shape[0]
    for gi, (o_ref, l_ref, os_ref) in enumerate(((o1_ref, l1_ref, os1_ref), (o2_ref, l2_ref, os2_ref))):
        dil = C_DILATIONS[gi + 1]
        n = tm // dil
        for r in range(dil):
            rows = pl.ds(r, n, stride=dil)
            ls_ref[gi, rows, :] = l_ref[r]
            for c in range(NSLAB):
                os_ref[c, rows, :] = o_ref[r, :, c * LANES:(c + 1) * LANES].astype(F32)
    l0, l1, l2 = l0_ref[...], ls_ref[0], ls_ref[1]
    mx = jnp.maximum(jnp.maximum(l0, l1), l2)
    es = [jnp.exp(l - mx) for l in (l0, l1, l2)]
    den = es[0] + es[1] + es[2]
    spread = (lax.broadcasted_iota(jnp.int32, (LANES, CB), 0)
              == (lax.broadcasted_iota(jnp.int32, (LANES, CB), 1) // C_HEAD_DIM) * LSE_LANES).astype(BF16)
    wide = []
    for e in es:
        w = e / den
        hi = w.astype(BF16)
        lo = (w - hi.astype(F32)).astype(BF16)
        wide.append(jnp.dot(hi, spread, preferred_element_type=F32) + jnp.dot(lo, spread, preferred_element_type=F32))
    for c in range(NSLAB):
        cols = slice(c * LANES, (c + 1) * LANES)
        mix = (wide[0][:, cols] * o0_ref[:, cols].astype(F32) + wide[1][:, cols] * os1_ref[c]
               + wide[2][:, cols] * os2_ref[c])
        y_ref[:, cols] = (mix * cz_ref[:, cols].astype(F32)).astype(BF16)


def _combine(res, misc, batch, seq, tm):
    (o0, l0), (o1, l1), (o2, l2) = res
    nseq = seq // tm
    def spec(dil, width):
        if dil == 1:
            return pl.BlockSpec((None, None, tm, width), lambda i: (i // nseq, 0, i % nseq, 0))
        return pl.BlockSpec((None, dil, tm // dil, width), lambda i: (i // nseq, 0, i % nseq, 0))

    in_specs = [spec(dil, width) for dil in C_DILATIONS for width in (CB, LANES)]
    return pl.pallas_call(
        _combine_kernel,
        grid=(batch * nseq,),
        in_specs=in_specs + [pl.BlockSpec((tm, CB), lambda i: (i, MISC_CZ))],
        out_specs=pl.BlockSpec((tm, CB), lambda i: (i, 0)),
        out_shape=jax.ShapeDtypeStruct((batch * seq, CB), BF16),
        scratch_shapes=[pltpu.VMEM((NSLAB, tm, LANES), F32), pltpu.VMEM((NSLAB, tm, LANES), F32),
                        pltpu.VMEM((C_NG - 1, tm, LANES), F32)],
        compiler_params=_params("parallel"),
        name="combine",
    )(o0, l0, o1, l1, o2, l2, misc)


def _merge_kernel(x_ref, ya_ref, yb_ref, yc_ref, wg0_ref, wg1_ref, wg2_ref, wb_ref, o_ref, *xb_scratch):
    if xb_scratch:
        @pl.when(pl.program_id(1) == 0)
        def _():
            xb_scratch[0][...] = x_ref[...].astype(BF16)

    xb = xb_scratch[0][...] if xb_scratch else x_ref[...]
    acc = None
    for kk, (y_ref, wg_ref) in enumerate(((ya_ref, wg0_ref), (yb_ref, wg1_ref), (yc_ref, wg2_ref))):
        gate = jax.nn.sigmoid(jnp.dot(xb, wg_ref[...], preferred_element_type=F32))
        term = gate * jnp.dot(y_ref[...], wb_ref[kk], preferred_element_type=F32)
        acc = term if acc is None else acc + term
    o_ref[...] = acc.astype(BF16)


def _merge(x2, ya, yb, yc, w_all, wbr_all, layer, tm):
    m = x2.shape[0]
    ncol = D_MODEL // CB
    ybs = pl.BlockSpec((tm, CB), lambda i, c: (i, 0))

    def gate_spec(kk):
        return pl.BlockSpec((None, D_MODEL, CB), lambda i, c: (layer, 0, OFF_G // CB + kk * ncol + c))

    return pl.pallas_call(
        _merge_kernel,
        grid=(m // tm, ncol),
        in_specs=[pl.BlockSpec((tm, D_MODEL), lambda i, c: (i, 0)), ybs, ybs, ybs,
                  gate_spec(0), gate_spec(1), gate_spec(2),
                  pl.BlockSpec((None, N_BRANCH, W_BR, CB), lambda i, c: (layer, 0, 0, c))],
        out_specs=pl.BlockSpec((tm, CB), lambda i, c: (i, c)),
        out_shape=jax.ShapeDtypeStruct((m, D_MODEL), BF16),
        scratch_shapes=[] if x2.dtype == BF16 else [pltpu.VMEM((tm, D_MODEL), BF16)],
        compiler_params=_params("parallel", "arbitrary"),
        name="merge",
    )(x2, ya, yb, yc, w_all, w_all, w_all, wbr_all)


def _outproj_kernel(x_ref, m_ref, wo_ref, g_ref, b_ref, o_ref, *ob_ref):
    for rc in range(x_ref.shape[0] // ROW_CHUNK):
        rows = slice(rc * ROW_CHUNK, (rc + 1) * ROW_CHUNK)
        out = jnp.dot(m_ref[rows, :], wo_ref[...], preferred_element_type=F32)
        z = ALPHA * x_ref[rows, :] + out
        mu = jnp.mean(z, axis=-1, keepdims=True)
        zc = z - mu
        var = jnp.mean(zc * zc, axis=-1, keepdims=True)
        y = zc * lax.rsqrt(var + LN_EPS) * g_ref[...] + b_ref[...]
        o_ref[rows, :] = y
        if ob_ref:
            ob_ref[0][rows, :] = y.astype(BF16)


def _outproj(x2, merged, wo_all, layer, ln_g, ln_b, tm, with_bf16):
    m = x2.shape[0]
    row = pl.BlockSpec((tm, D_MODEL), lambda i: (i, 0))
    vec = pl.BlockSpec((1, D_MODEL), lambda i: (0, 0))
    return pl.pallas_call(
        _outproj_kernel,
        grid=(m // tm,),
        in_specs=[row, row,
                  pl.BlockSpec((None, D_MODEL, D_MODEL), lambda i: (layer, 0, 0), pipeline_mode=pl.Buffered(1)),
                  vec, vec],
        out_specs=[row, row] if with_bf16 else row,
        out_shape=([jax.ShapeDtypeStruct((m, D_MODEL), F32), jax.ShapeDtypeStruct((m, D_MODEL), BF16)]
                   if with_bf16 else jax.ShapeDtypeStruct((m, D_MODEL), F32)),
        compiler_params=_params("parallel"),
        name="outproj",
    )(x2, merged, wo_all, ln_g, ln_b)


def _rope_table(seq):
    half = ROPE_DIM // 2
    inv_freq = ROPE_THETA ** (-jnp.arange(half, dtype=F32) / half)
    ang = jnp.arange(seq, dtype=F32)[:, None] * inv_freq[None, :]
    cos, sin = jnp.cos(ang), jnp.sin(ang)
    ones = jnp.ones((seq, C_HEAD_DIM - 2 * ROPE_DIM), F32)
    head = jnp.concatenate([cos, cos, -sin, sin, ones], axis=1)
    return jnp.tile(head, (1, LANES // C_HEAD_DIM))


TM_INPROJ = 1024
TM_SGU = 512
TM_COMBINE = 1024
TM_MERGE = 1024
TM_OUTPROJ = 512


def _layer(x2, x_mm, batch, seq, wl, rope_tab, dft_t):
    h_a, misc, g0, g1, g2 = _inproj(x_mm, wl["w_in"], wl["layer"], rope_tab, batch, seq, TM_INPROJ)
    y_a = _fourier(h_a, misc, *dft_t, wl["w_amap"], batch, seq)
    y_b = _sgu(misc, wl["sgu_ln_g"], wl["sgu_ln_b"], wl["sgu_w"], wl["sgu_bias"], TM_SGU)
    groups = (g0.reshape(batch, 1, seq, 3 * CB), g1, g2)
    y_c = _combine([_attn_group(qkv, g) for g, qkv in enumerate(groups)], misc, batch, seq, TM_COMBINE)
    merged = _merge(x_mm, y_a, y_b, y_c, wl["w_in"], wl["w_branch"], wl["layer"], TM_MERGE)
    last = wl["layer"] == DEPTH - 1
    res = _outproj(x2, merged, wl["w_out"], wl["layer"], wl["ln_g"], wl["ln_b"], TM_OUTPROJ, not last)
    return (res, res) if last else tuple(res)


def kernel(x_prompt, x_sample, w_in, w_amap, sgu_ln_g, sgu_ln_b, sgu_w, sgu_b, w_branch, w_out, ln_g, ln_b):
    w_in_bf, w_branch_bf, w_out_bf = w_in.astype(BF16), w_branch.astype(BF16), w_out.astype(BF16)
    layers = []
    for l in range(DEPTH):
        layers.append({
            "layer": l,
            "w_in": w_in_bf,
            "w_amap": w_amap[l].astype(BF16),
            "sgu_ln_g": sgu_ln_g[l][None, :],
            "sgu_ln_b": sgu_ln_b[l][None, :],
            "sgu_w": sgu_w[l].astype(BF16),
            "sgu_bias": jnp.repeat(sgu_b[l].T, GDIM, axis=1),
            "w_branch": w_branch_bf,
            "w_out": w_out_bf,
            "ln_g": ln_g[l][None, :],
            "ln_b": ln_b[l][None, :],
        })
    outs = []
    for x in (x_prompt, x_sample):
        batch, seq, _ = x.shape
        rope_t = _rope_table(seq)
        dft_t = _dft_tables(seq)
        x2 = x_mm = x.reshape(batch * seq, D_MODEL)
        for wl in layers:
            x2, x_mm = _layer(x2, x_mm, batch, seq, wl, rope_t, dft_t)
        outs.append(x2.reshape(batch, seq, D_MODEL))
    return tuple(outs)
```

```python
import functools
import math

import jax
import jax.numpy as jnp
from jax import lax
from jax.experimental import pallas as pl
from jax.experimental.pallas import tpu as pltpu

F32 = jnp.float32
BF16 = jnp.bfloat16

D_MODEL = 2048
DEPTH = 2
W_BR = 512
A_GROUPS = 4
GDIM = 128
B_CHUNK = 128
C_DILATIONS = (1, 4, 16)
C_NG = len(C_DILATIONS)
C_HEADS = 8
C_HEAD_DIM = 64
C_QKV = C_NG * C_HEADS * C_HEAD_DIM
ROPE_DIM = C_HEAD_DIM // 4
ROPE_THETA = 500000.0
N_BRANCH = 3
ALPHA = (2 * DEPTH) ** 0.25
LN_EPS = 1e-5
NEG_BIG = -1e30
HALF_WIN = 64
Q_SCALE = math.log2(math.e) / math.sqrt(C_HEAD_DIM)

LANES = 128
BF16_ROWS = 16
CB = 512
NSLAB = CB // LANES
OFF_A, OFF_AZ, OFF_B, OFF_BZ, OFF_C = 0, 512, 1024, 2048, 2560
OFF_CZ = OFF_C + 3 * C_QKV
OFF_G = OFF_CZ + W_BR
N_MIX_BLOCKS = 15
MISC_AZ, MISC_BZ, MISC_CZ, MISC_U, MISC_V = 0, 1, 2, 3, 4
N_MISC = 5
J_MISC = 1
J_QKV = J_MISC + N_MISC

VMEM_LIMIT = 56 * 1024 * 1024


def _params(*sem):
    return pltpu.CompilerParams(dimension_semantics=sem, vmem_limit_bytes=VMEM_LIMIT)


def _slabs(x):
    return [x[:, c * LANES:(c + 1) * LANES] for c in range(x.shape[1] // LANES)]


def _silu(h):
    return h * jax.nn.sigmoid(h)


def _gelu_tanh(h):
    c = math.sqrt(2.0 / math.pi)
    return h * (0.5 * (1.0 + jnp.tanh(c * (h + 0.044715 * (h * h * h)))))


def _rope_factors(t):
    lane = lax.broadcasted_iota(jnp.int32, t.shape, 1) & (C_HEAD_DIM - 1)
    cosf = jnp.where((lane >= ROPE_DIM) & (lane < 2 * ROPE_DIM), 1.0, t)
    sinf = jnp.where(lane < ROPE_DIM, pltpu.roll(t, LANES - ROPE_DIM, axis=1), 0.0)
    return cosf, sinf, lane < ROPE_DIM // 2


def _rope_slab(hc, cosf, sinf, first_half):
    up = pltpu.roll(hc, LANES - ROPE_DIM // 2, axis=1)
    dn = pltpu.roll(hc, ROPE_DIM // 2, axis=1)
    return hc * cosf + jnp.where(first_half, up, dn) * sinf


ROW_CHUNK = 256


def _inproj_kernel(x_ref, w_ref, tab_ref, ha_ref, misc_ref, g0_ref, g1_ref, g2_ref, tabs_ref, acc_ref):
    j = pl.program_id(1)
    tm = x_ref.shape[0]

    @pl.when(j == 0)
    def _():
        tabs_ref[0] = tab_ref[...]
        for g, dil in enumerate(C_DILATIONS[1:], start=1):
            n = tm // dil
            for r in range(dil):
                tabs_ref[g, r * n:(r + 1) * n, :] = tab_ref[pl.ds(r, n, stride=dil), :]

    def chunks(size=ROW_CHUNK):
        for rc in range(tm // size):
            rows = slice(rc * size, (rc + 1) * size)
            yield rc, rows, jnp.dot(x_ref[rows, :], w_ref[...], preferred_element_type=F32)

    def fourier_in():
        for _, rows, res in chunks():
            ha_ref[rows, :] = res

    def gate_path():
        for _, rows, res in chunks():
            misc_ref[rows, :] = _silu(res).astype(BF16)

    def gating_in():
        for _, rows, res in chunks():
            misc_ref[rows, :] = _gelu_tanh(res).astype(BF16)

    def qkv(g):
        dil, out_ref = C_DILATIONS[g], (g0_ref, g1_ref, g2_ref)[g]
        part = j - (J_QKV + 3 * g)
        is_v = part == 2
        scale = jnp.where(part == 0, Q_SCALE, 1.0).astype(F32)
        size = max(ROW_CHUNK, BF16_ROWS * dil)
        n = size // dil
        for rc, rows, res in chunks(size):
            if dil > 1:
                for c, slab in enumerate(_slabs(res)):
                    acc_ref[rc % 2, c, :size, :] = slab
            for r in range(dil):
                cosf, sinf, first_half = _rope_factors(tabs_ref[g, pl.ds(r * (tm // dil) + rc * n, n), :])
                cosf = jnp.where(is_v, 1.0, cosf) * scale
                sinf = jnp.where(is_v, 0.0, sinf) * scale
                for c in range(NSLAB):
                    cols = slice(c * LANES, (c + 1) * LANES)
                    if dil > 1:
                        hc = acc_ref[rc % 2, c, pl.ds(r, n, stride=dil), :]
                    else:
                        hc = res[:, cols]
                    hc = _rope_slab(hc, cosf, sinf, first_half)
                    if dil > 1:
                        out_ref[r, rc * n:(rc + 1) * n, cols] = hc.astype(BF16)
                    else:
                        out_ref[rows, cols] = hc.astype(BF16)

    leaves = [(0, fourier_in), (J_MISC + MISC_AZ, gate_path), (J_MISC + MISC_U, gating_in)]
    leaves += [(J_QKV + 3 * g, functools.partial(qkv, g)) for g in range(C_NG)]

    def dispatch(sub):
        if len(sub) == 1:
            sub[0][1]()
            return
        mid = len(sub) // 2
        pl.when(j < sub[mid][0])(lambda: dispatch(sub[:mid]))
        pl.when(j >= sub[mid][0])(lambda: dispatch(sub[mid:]))

    dispatch(leaves)


MIX_BLOCK_OF = tuple(off // CB for off in (
    [OFF_A, OFF_AZ, OFF_BZ, OFF_CZ, OFF_B, OFF_B + CB]
    + [OFF_C + part * C_QKV + g * CB for g in range(C_NG) for part in range(3)]))


def _mix_block(j):
    blk = jnp.int32(MIX_BLOCK_OF[0])
    for step in range(1, N_MIX_BLOCKS):
        blk = jnp.where(j == step, MIX_BLOCK_OF[step], blk)
    return blk


def _inproj(x2, w_all, layer, rope_tab, batch, seq, tm):
    m = x2.shape[0]
    nseq = seq // tm

    def part(j, first):
        return jnp.clip(j - first, 0, 2)

    def dil_spec(g):
        dil = C_DILATIONS[g]
        return pl.BlockSpec((None, dil, tm // dil, CB),
                            lambda i, j: (i // nseq, 0, i % nseq, part(j, J_QKV + 3 * g)))

    def dil_shape(g):
        dil = C_DILATIONS[g]
        return jax.ShapeDtypeStruct((batch, dil, seq // dil, 3 * CB), BF16)

    return pl.pallas_call(
        _inproj_kernel,
        grid=(m // tm, N_MIX_BLOCKS),
        in_specs=[
            pl.BlockSpec((tm, D_MODEL), lambda i, j: (i, 0)),
            pl.BlockSpec((None, D_MODEL, CB), lambda i, j: (layer, 0, _mix_block(j))),
            pl.BlockSpec((tm, LANES), lambda i, j: (i % nseq, 0)),
        ],
        out_specs=[
            pl.BlockSpec((tm, CB), lambda i, j: (i, 0)),
            pl.BlockSpec((tm, CB), lambda i, j: (i, jnp.clip(j - J_MISC, 0, N_MISC - 1))),
            pl.BlockSpec((tm, CB), lambda i, j: (i, part(j, J_QKV))),
            dil_spec(1), dil_spec(2),
        ],
        out_shape=[
            jax.ShapeDtypeStruct((m, CB), F32),
            jax.ShapeDtypeStruct((m, N_MISC * CB), BF16),
            jax.ShapeDtypeStruct((m, 3 * CB), BF16),
            dil_shape(1), dil_shape(2),
        ],
        scratch_shapes=[pltpu.VMEM((C_NG, tm, LANES), F32), pltpu.VMEM((2, NSLAB, ROW_CHUNK, LANES), F32)],
        compiler_params=_params("parallel", "arbitrary"),
        name="inproj",
    )(x2, w_all, rope_tab)


def _sgu_kernel(u_ref, v_ref, z_ref, g_ref, b_ref, ws_ref, bs_ref, o_ref):
    tm = u_ref.shape[0]
    v = v_ref[...].astype(F32)
    mu = jnp.mean(v, axis=-1, keepdims=True)
    vc = v - mu
    var = jnp.mean(vc * vc, axis=-1, keepdims=True)
    vn = (vc * lax.rsqrt(var + LN_EPS) * g_ref[...] + b_ref[...]).astype(BF16)
    bias = bs_ref[...]
    nchunk = tm // B_CHUNK
    mixed = []
    for g in range(A_GROUPS):
        cols = slice(g * GDIM, (g + 1) * GDIM)
        side = jnp.concatenate([vn[n * B_CHUNK:(n + 1) * B_CHUNK, cols] for n in range(nchunk)], axis=1)
        mixed.append(jnp.dot(ws_ref[g], side, preferred_element_type=F32))
    for n in range(nchunk):
        rows = slice(n * B_CHUNK, (n + 1) * B_CHUNK)
        mix = jnp.concatenate([mixed[g][:, n * GDIM:(n + 1) * GDIM] for g in range(A_GROUPS)], axis=1) + bias
        o_ref[rows, :] = (u_ref[rows, :].astype(F32) * mix * z_ref[rows, :].astype(F32)).astype(BF16)


def _sgu(misc, ln_g, ln_b, ws_bf, bias_full, tm):
    m = misc.shape[0]
    col = lambda blk: pl.BlockSpec((tm, CB), lambda i: (i, blk))
    full = lambda shape: pl.BlockSpec(shape, lambda i: (0,) * len(shape))
    return pl.pallas_call(
        _sgu_kernel,
        grid=(m // tm,),
        in_specs=[col(MISC_U), col(MISC_V), col(MISC_BZ), full((1, CB)), full((1, CB)),
                  full((A_GROUPS, B_CHUNK, B_CHUNK)), full((B_CHUNK, CB))],
        out_specs=pl.BlockSpec((tm, CB), lambda i: (i, 0)),
        out_shape=jax.ShapeDtypeStruct((m, CB), BF16),
        compiler_params=_params("parallel"),
        name="sgu",
    )(misc, misc, misc, ln_g, ln_b, ws_bf, bias_full)


SA_ROWS = 8
SA_BLOCKS = 4
SB_ROWS = BF16_ROWS


def _fft_a_kernel(x0_ref, x1_ref, x2_ref, x3_ref, f_ref, y_ref, xin_ref, yout_ref):
    n1 = x0_ref.shape[0]
    for blk in range(SA_BLOCKS):
        srows = slice(blk * SA_ROWS, (blk + 1) * SA_ROWS)
        for g, x_ref in enumerate((x0_ref, x1_ref, x2_ref, x3_ref)):
            xin_ref[blk, g] = x_ref[:, srows, :].reshape(n1 * SA_ROWS, LANES)
        for jj in range(SA_ROWS):
            xj = jnp.concatenate(
                [xin_ref[blk, g, pl.ds(jj, n1, stride=SA_ROWS), :] for g in range(A_GROUPS)],
                axis=1).astype(BF16)
            yj = jnp.dot(f_ref[...], xj, preferred_element_type=F32)
            for g, slab in enumerate(_slabs(yj)):
                yout_ref[blk, g, pl.ds(jj, 2 * n1, stride=SA_ROWS), :] = slab
        for g in range(A_GROUPS):
            y_ref[g, :, srows, :] = yout_ref[blk, g].reshape(2 * n1, SA_ROWS, LANES)


def _fft_b_kernel(yr_ref, yi_ref, m_ref, z_ref, cs_ref, wm_ref, o_ref, f_ref, csw_ref, *, scale):
    n2 = yr_ref.shape[2]
    for g in range(A_GROUPS):
        csw_ref[g] = (jnp.dot(cs_ref[...], wm_ref[g], preferred_element_type=F32) * scale).astype(BF16)

    def body(kk, carry):
        ycat = jnp.concatenate(
            [jnp.concatenate([y_ref[g, kk] for g in range(A_GROUPS)], axis=1) for y_ref in (yr_ref, yi_ref)],
            axis=0).astype(BF16)
        zz = jnp.dot(m_ref[kk], ycat, preferred_element_type=F32).astype(BF16)
        for g in range(A_GROUPS):
            cols = slice(g * GDIM, (g + 1) * GDIM)
            z_ri = jnp.concatenate([zz[:n2, cols], zz[n2:, cols]], axis=1)
            f_ref[g, pl.ds(kk, n2, stride=SB_ROWS), :] = jnp.dot(
                z_ri, csw_ref[g], preferred_element_type=F32)
        return carry

    lax.fori_loop(0, SB_ROWS, body, 0)
    mixed = jnp.concatenate([f_ref[g] for g in range(A_GROUPS)], axis=1)
    gate = z_ref[...].reshape(n2 * SB_ROWS, CB).astype(F32)
    o_ref[...] = (mixed * gate).astype(BF16).reshape(n2, SB_ROWS, CB)


def _fourier(h_a, misc, f1_tab, m_tab, cs_tab, wmap_bf, batch, seq):
    n2 = B_CHUNK
    n1 = seq // n2
    x4 = h_a.reshape(batch, n1, n2, CB)
    step_rows = SA_ROWS * SA_BLOCKS
    xspec = lambda g: pl.BlockSpec((None, n1, step_rows, LANES), lambda b, t: (b, 0, t, g))
    y = pl.pallas_call(
        _fft_a_kernel,
        grid=(batch, n2 // step_rows),
        in_specs=[xspec(0), xspec(1), xspec(2), xspec(3),
                  pl.BlockSpec((2 * n1, n1), lambda b, t: (0, 0))],
        out_specs=pl.BlockSpec((None, A_GROUPS, 2 * n1, step_rows, LANES), lambda b, t: (b, 0, 0, t, 0)),
        out_shape=jax.ShapeDtypeStruct((batch, A_GROUPS, 2 * n1, n2, LANES), F32),
        scratch_shapes=[pltpu.VMEM((SA_BLOCKS, A_GROUPS, n1 * SA_ROWS, LANES), F32),
                        pltpu.VMEM((SA_BLOCKS, A_GROUPS, 2 * n1 * SA_ROWS, LANES), F32)],
        compiler_params=_params("parallel", "parallel"),
        name="fft_a",
    )(x4, x4, x4, x4, f1_tab)
    nk = n1 // SB_ROWS
    z4 = misc.reshape(batch, n2, n1, N_MISC * CB)
    out = pl.pallas_call(
        functools.partial(_fft_b_kernel, scale=1.0 / math.sqrt(seq * GDIM)),
        grid=(nk, batch),
        in_specs=[pl.BlockSpec((None, A_GROUPS, SB_ROWS, n2, LANES), lambda k, b: (b, 0, k, 0, 0)),
                  pl.BlockSpec((None, A_GROUPS, SB_ROWS, n2, LANES), lambda k, b: (b, 0, nk + k, 0, 0)),
                  pl.BlockSpec((SB_ROWS, 2 * n2, 2 * n2), lambda k, b: (k, 0, 0)),
                  pl.BlockSpec((None, n2, SB_ROWS, CB), lambda k, b: (b, 0, k, MISC_AZ)),
                  pl.BlockSpec((2 * GDIM, GDIM), lambda k, b: (0, 0)),
                  pl.BlockSpec((A_GROUPS, GDIM, GDIM), lambda k, b: (0, 0, 0))],
        out_specs=pl.BlockSpec((None, n2, SB_ROWS, CB), lambda k, b: (b, 0, k, 0)),
        out_shape=jax.ShapeDtypeStruct((batch, n2, n1, CB), BF16),
        scratch_shapes=[pltpu.VMEM((A_GROUPS, n2 * SB_ROWS, LANES), F32),
                        pltpu.VMEM((A_GROUPS, 2 * GDIM, GDIM), BF16)],
        compiler_params=_params("parallel", "parallel"),
        name="fft_b",
    )(y, y, m_tab, z4, cs_tab, wmap_bf)
    return out.reshape(batch * seq, CB)


def _dft_tables(seq):
    n2 = B_CHUNK
    n1 = seq // n2
    two_pi = 2.0 * math.pi

    def cs(idx, period):
        ang = (idx % period).astype(F32) * (two_pi / period)
        return jnp.cos(ang), jnp.sin(ang)

    i1 = jnp.arange(n1, dtype=jnp.int32)
    c1, s1 = cs(i1[:, None] * i1[None, :], n1)
    f1_tab = jnp.concatenate([c1, -s1], axis=0).astype(BF16)
    k = i1[:, None, None] + n1 * jnp.arange(n2, dtype=jnp.int32)[None, :, None]
    s2 = jnp.arange(n2, dtype=jnp.int32)[None, None, :]
    cm, sm = cs(k * s2, seq)
    m_tab = jnp.concatenate([jnp.concatenate([cm, sm], axis=2),
                             jnp.concatenate([-sm, cm], axis=2)], axis=1).astype(BF16)
    ic = jnp.arange(GDIM, dtype=jnp.int32)
    cc, sc = cs(ic[:, None] * ic[None, :], GDIM)
    cs_tab = jnp.concatenate([cc, sc], axis=0).astype(BF16)
    return f1_tab, m_tab, cs_tab


Q_SUB = 128
LSE_LANES = LANES // C_HEADS
ATTN_ROWS = 1024


def _attn_kernel(q_ref, kc_ref, kp_ref, kn_ref, vc_ref, vp_ref, vn_ref, o_ref, l_ref, *, seq_len, tq):
    t = pl.program_id(2)
    tk = Q_SUB + 2 * HALF_WIN
    ones = jnp.ones((tk, LANES), BF16)
    row = lax.broadcasted_iota(jnp.int32, (2 * Q_SUB, tk), 0) & (Q_SUB - 1)
    col = lax.broadcasted_iota(jnp.int32, (2 * Q_SUB, tk), 1)
    band = jnp.abs(row + HALF_WIN - col) <= HALF_WIN
    lane = lax.broadcasted_iota(jnp.int32, (Q_SUB, LANES), 1)
    lo = lane < C_HEAD_DIM
    head_of_lane = lane // LSE_LANES
    for rr in range(q_ref.shape[0]):
        k = jnp.concatenate([kp_ref[rr], kc_ref[rr], kn_ref[rr]], axis=0)
        v = jnp.concatenate([vp_ref[rr], vc_ref[rr], vn_ref[rr]], axis=0)
        for u in range(tq // Q_SUB):
            rows = slice(u * Q_SUB, (u + 1) * Q_SUB)
            kpos = t * tq + u * Q_SUB - HALF_WIN + col[:1]
            valid = band & ((kpos >= 0) & (kpos < seq_len))
            q = q_ref[rr, rows, :]
            ku, vu = k[u * Q_SUB:u * Q_SUB + tk], v[u * Q_SUB:u * Q_SUB + tk]
            lse_all = jnp.zeros((Q_SUB, LANES), F32)
            for hp in range(C_HEADS // 2):
                cols = slice(hp * LANES, (hp + 1) * LANES)
                qp = q[:, cols]
                zero = jnp.zeros_like(qp)
                q2 = jnp.concatenate([jnp.where(lo, qp, zero), jnp.where(lo, zero, qp)], axis=0)
                s = lax.dot_general(q2, ku[:, cols], (((1,), (1,)), ((), ())), preferred_element_type=F32)
                s = jnp.where(valid, s, NEG_BIG)
                m = jnp.max(s, axis=1, keepdims=True)
                p = jnp.exp2(s - m).astype(BF16)
                pv = jnp.dot(p, jnp.concatenate([vu[:, cols], ones], axis=1), preferred_element_type=F32)
                den = pv[:, LANES:]
                o = pv[:, :LANES] / den
                lse = m * math.log(2.0) + jnp.log(den)
                o_ref[rr, rows, cols] = jnp.where(lo, o[:Q_SUB], o[Q_SUB:]).astype(BF16)
                lse_all = jnp.where(head_of_lane == 2 * hp, lse[:Q_SUB], lse_all)
                lse_all = jnp.where(head_of_lane == 2 * hp + 1, lse[Q_SUB:], lse_all)
            l_ref[rr, rows, :] = lse_all


def _attn_group(qkv, g):
    batch, dil, ln, _ = qkv.shape
    tq = min(ATTN_ROWS, ln)
    nres = min(dil, ATTN_ROWS // tq)
    per = tq // HALF_WIN
    nhalo = ln // HALF_WIN

    def cur(blk):
        return pl.BlockSpec((None, nres, tq, CB), lambda b, r, t: (b, r, t, blk))

    def before(blk):
        return pl.BlockSpec((None, nres, HALF_WIN, CB),
                            lambda b, r, t: (b, r, jnp.maximum(t * per - 1, 0), blk))

    def after(blk):
        return pl.BlockSpec((None, nres, HALF_WIN, CB),
                            lambda b, r, t: (b, r, jnp.minimum((t + 1) * per, nhalo - 1), blk))

    tile = pl.BlockSpec((None, nres, tq, CB), lambda b, r, t: (b, r, t, 0))
    ltile = pl.BlockSpec((None, nres, tq, LANES), lambda b, r, t: (b, r, t, 0))
    return pl.pallas_call(
        functools.partial(_attn_kernel, seq_len=ln, tq=tq),
        grid=(batch, dil // nres, ln // tq),
        in_specs=[cur(0), cur(1), before(1), after(1), cur(2), before(2), after(2)],
        out_specs=[tile, ltile],
        out_shape=[jax.ShapeDtypeStruct((batch, dil, ln, CB), BF16),
                   jax.ShapeDtypeStruct((batch, dil, ln, LANES), F32)],
        compiler_params=_params("parallel", "parallel", "parallel"),
        name=f"attn{g}",
    )(*([qkv] * 7))


def _combine_kernel(o0_ref, l0_ref, o1_ref, l1_ref, o2_ref, l2_ref, cz_ref, y_ref, os1_ref, os2_ref, ls_ref):
    tm = o0_ref.shape[0]
    for gi, (o_ref, l_ref, os_ref) in enumerate(((o1_ref, l1_ref, os1_ref), (o2_ref, l2_ref, os2_ref))):
        dil = C_DILATIONS[gi + 1]
        n = tm // dil
        for r in range(dil):
            rows = pl.ds(r, n, stride=dil)
            ls_ref[gi, rows, :] = l_ref[r]
            for c in range(NSLAB):
                os_ref[c, rows, :] = o_ref[r, :, c * LANES:(c + 1) * LANES].astype(F32)
    l0, l1, l2 = l0_ref[...], ls_ref[0], ls_ref[1]
    mx = jnp.maximum(jnp.maximum(l0, l1), l2)
    es = [jnp.exp(l - mx) for l in (l0, l1, l2)]
    den = es[0] + es[1] + es[2]
    spread = (lax.broadcasted_iota(jnp.int32, (LANES, CB), 0)
              == (lax.broadcasted_iota(jnp.int32, (LANES, CB), 1) // C_HEAD_DIM) * LSE_LANES).astype(BF16)
    wide = []
    for e in es:
        w = e / den
        hi = w.astype(BF16)
        lo = (w - hi.astype(F32)).astype(BF16)
        wide.append(jnp.dot(hi, spread, preferred_element_type=F32) + jnp.dot(lo, spread, preferred_element_type=F32))
    for c in range(NSLAB):
        cols = slice(c * LANES, (c + 1) * LANES)
        mix = (wide[0][:, cols] * o0_ref[:, cols].astype(F32) + wide[1][:, cols] * os1_ref[c]
               + wide[2][:, cols] * os2_ref[c])
        y_ref[:, cols] = (mix * cz_ref[:, cols].astype(F32)).astype(BF16)


def _combine(res, misc, batch, seq, tm):
    (o0, l0), (o1, l1), (o2, l2) = res
    nseq = seq // tm
    def spec(dil, width):
        if dil == 1:
            return pl.BlockSpec((None, None, tm, width), lambda i: (i // nseq, 0, i % nseq, 0))
        return pl.BlockSpec((None, dil, tm // dil, width), lambda i: (i // nseq, 0, i % nseq, 0))

    in_specs = [spec(dil, width) for dil in C_DILATIONS for width in (CB, LANES)]
    return pl.pallas_call(
        _combine_kernel,
        grid=(batch * nseq,),
        in_specs=in_specs + [pl.BlockSpec((tm, CB), lambda i: (i, MISC_CZ))],
        out_specs=pl.BlockSpec((tm, CB), lambda i: (i, 0)),
        out_shape=jax.ShapeDtypeStruct((batch * seq, CB), BF16),
        scratch_shapes=[pltpu.VMEM((NSLAB, tm, LANES), F32), pltpu.VMEM((NSLAB, tm, LANES), F32),
                        pltpu.VMEM((C_NG - 1, tm, LANES), F32)],
        compiler_params=_params("parallel"),
        name="combine",
    )(o0, l0, o1, l1, o2, l2, misc)


def _merge_kernel(x_ref, ya_ref, yb_ref, yc_ref, wg0_ref, wg1_ref, wg2_ref, wb_ref, o_ref):
    xb = x_ref[...]
    acc = None
    for kk, (y_ref, wg_ref) in enumerate(((ya_ref, wg0_ref), (yb_ref, wg1_ref), (yc_ref, wg2_ref))):
        gate = jax.nn.sigmoid(jnp.dot(xb, wg_ref[...], preferred_element_type=F32))
        term = gate * jnp.dot(y_ref[...], wb_ref[kk], preferred_element_type=F32)
        acc = term if acc is None else acc + term
    o_ref[...] = acc.astype(BF16)


def _merge(x2, ya, yb, yc, w_all, wbr_all, layer, tm):
    m = x2.shape[0]
    ncol = D_MODEL // CB
    ybs = pl.BlockSpec((tm, CB), lambda i, c: (i, 0))

    def gate_spec(kk):
        return pl.BlockSpec((None, D_MODEL, CB), lambda i, c: (layer, 0, OFF_G // CB + kk * ncol + c))

    return pl.pallas_call(
        _merge_kernel,
        grid=(m // tm, ncol),
        in_specs=[pl.BlockSpec((tm, D_MODEL), lambda i, c: (i, 0)), ybs, ybs, ybs,
                  gate_spec(0), gate_spec(1), gate_spec(2),
                  pl.BlockSpec((None, N_BRANCH, W_BR, CB), lambda i, c: (layer, 0, 0, c))],
        out_specs=pl.BlockSpec((tm, CB), lambda i, c: (i, c)),
        out_shape=jax.ShapeDtypeStruct((m, D_MODEL), BF16),
        compiler_params=_params("parallel", "parallel"),
        name="merge",
    )(x2, ya, yb, yc, w_all, w_all, w_all, wbr_all)


def _outproj_kernel(x_ref, m_ref, wo_ref, g_ref, b_ref, o_ref, *ob_ref):
    for rc in range(x_ref.shape[0] // ROW_CHUNK):
        rows = slice(rc * ROW_CHUNK, (rc + 1) * ROW_CHUNK)
        out = jnp.dot(m_ref[rows, :], wo_ref[...], preferred_element_type=F32)
        z = ALPHA * x_ref[rows, :] + out
        mu = jnp.mean(z, axis=-1, keepdims=True)
        zc = z - mu
        var = jnp.mean(zc * zc, axis=-1, keepdims=True)
        y = zc * lax.rsqrt(var + LN_EPS) * g_ref[...] + b_ref[...]
        o_ref[rows, :] = y
        if ob_ref:
            ob_ref[0][rows, :] = y.astype(BF16)


def _outproj(x2, merged, wo_all, layer, ln_g, ln_b, tm, with_bf16):
    m = x2.shape[0]
    row = pl.BlockSpec((tm, D_MODEL), lambda i: (i, 0))
    vec = pl.BlockSpec((1, D_MODEL), lambda i: (0, 0))
    return pl.pallas_call(
        _outproj_kernel,
        grid=(m // tm,),
        in_specs=[row, row,
                  pl.BlockSpec((None, D_MODEL, D_MODEL), lambda i: (layer, 0, 0), pipeline_mode=pl.Buffered(1)),
                  vec, vec],
        out_specs=[row, row] if with_bf16 else row,
        out_shape=([jax.ShapeDtypeStruct((m, D_MODEL), F32), jax.ShapeDtypeStruct((m, D_MODEL), BF16)]
                   if with_bf16 else jax.ShapeDtypeStruct((m, D_MODEL), F32)),
        compiler_params=_params("parallel"),
        name="outproj",
    )(x2, merged, wo_all, ln_g, ln_b)


def _rope_table(seq):
    half = ROPE_DIM // 2
    inv_freq = ROPE_THETA ** (-jnp.arange(half, dtype=F32) / half)
    ang = jnp.arange(seq, dtype=F32)[:, None] * inv_freq[None, :]
    cos, sin = jnp.cos(ang), jnp.sin(ang)
    ones = jnp.ones((seq, C_HEAD_DIM - 2 * ROPE_DIM), F32)
    head = jnp.concatenate([cos, cos, -sin, sin, ones], axis=1)
    return jnp.tile(head, (1, LANES // C_HEAD_DIM))


TM_INPROJ = 2048
TM_SGU = 512
TM_COMBINE = 1024
TM_MERGE = 1024
TM_OUTPROJ = 512


def _layer(x2, x_mm, batch, seq, wl, rope_tab, dft_t):
    h_a, misc, g0, g1, g2 = _inproj(x_mm, wl["w_in"], wl["layer"], rope_tab, batch, seq, TM_INPROJ)
    y_a = _fourier(h_a, misc, *dft_t, wl["w_amap"], batch, seq)
    y_b = _sgu(misc, wl["sgu_ln_g"], wl["sgu_ln_b"], wl["sgu_w"], wl["sgu_bias"], TM_SGU)
    groups = (g0.reshape(batch, 1, seq, 3 * CB), g1, g2)
    y_c = _combine([_attn_group(qkv, g) for g, qkv in enumerate(groups)], misc, batch, seq, TM_COMBINE)
    merged = _merge(x_mm, y_a, y_b, y_c, wl["w_in"], wl["w_branch"], wl["layer"], TM_MERGE)
    last = wl["layer"] == DEPTH - 1
    res = _outproj(x2, merged, wl["w_out"], wl["layer"], wl["ln_g"], wl["ln_b"], TM_OUTPROJ, not last)
    return (res, res) if last else tuple(res)


def kernel(x_prompt, x_sample, w_in, w_amap, sgu_ln_g, sgu_ln_b, sgu_w, sgu_b, w_branch, w_out, ln_g, ln_b):
    w_in_bf, w_branch_bf, w_out_bf = w_in.astype(BF16), w_branch.astype(BF16), w_out.astype(BF16)
    layers = []
    for l in range(DEPTH):
        layers.append({
            "layer": l,
            "w_in": w_in_bf,
            "w_amap": w_amap[l].astype(BF16),
            "sgu_ln_g": sgu_ln_g[l][None, :],
            "sgu_ln_b": sgu_ln_b[l][None, :],
            "sgu_w": sgu_w[l].astype(BF16),
            "sgu_bias": jnp.repeat(sgu_b[l].T, GDIM, axis=1),
            "w_branch": w_branch_bf,
            "w_out": w_out_bf,
            "ln_g": ln_g[l][None, :],
            "ln_b": ln_b[l][None, :],
        })
    outs = []
    for x in (x_prompt, x_sample):
        batch, seq, _ = x.shape
        rope_t = _rope_table(seq)
        dft_t = _dft_tables(seq)
        x2 = x.reshape(batch * seq, D_MODEL)
        x_mm = x2.astype(BF16)
        for wl in layers:
            x2, x_mm = _layer(x2, x_mm, batch, seq, wl, rope_t, dft_t)
        outs.append(x2.reshape(batch, seq, D_MODEL))
    return tuple(outs)
```

```python
import functools
import math

import jax
import jax.numpy as jnp
from jax import lax
from jax.experimental import pallas as pl
from jax.experimental.pallas import tpu as pltpu

F32 = jnp.float32
BF16 = jnp.bfloat16

D_MODEL = 2048
DEPTH = 2
W_BR = 512
A_GROUPS = 4
GDIM = 128
B_CHUNK = 128
C_DILATIONS = (1, 4, 16)
C_NG = len(C_DILATIONS)
C_HEADS = 8
C_HEAD_DIM = 64
C_QKV = C_NG * C_HEADS * C_HEAD_DIM
ROPE_DIM = C_HEAD_DIM // 4
ROPE_THETA = 500000.0
N_BRANCH = 3
ALPHA = (2 * DEPTH) ** 0.25
LN_EPS = 1e-5
NEG_BIG = -1e30
HALF_WIN = 64
Q_SCALE = math.log2(math.e) / math.sqrt(C_HEAD_DIM)

LANES = 128
BF16_ROWS = 16
CB = 512
NSLAB = CB // LANES
OFF_A, OFF_AZ, OFF_B, OFF_BZ, OFF_C = 0, 512, 1024, 2048, 2560
OFF_CZ = OFF_C + 3 * C_QKV
OFF_G = OFF_CZ + W_BR
N_MIX_BLOCKS = 15
MISC_AZ, MISC_BZ, MISC_CZ, MISC_U, MISC_V = 0, 1, 2, 3, 4
N_MISC = 5
J_MISC = 1
J_QKV = J_MISC + N_MISC

VMEM_LIMIT = 56 * 1024 * 1024


def _params(*sem):
    return pltpu.CompilerParams(dimension_semantics=sem, vmem_limit_bytes=VMEM_LIMIT)


def _slabs(x):
    return [x[:, c * LANES:(c + 1) * LANES] for c in range(x.shape[1] // LANES)]


def _silu(h):
    return h * jax.nn.sigmoid(h)


def _gelu_tanh(h):
    c = math.sqrt(2.0 / math.pi)
    return h * (0.5 * (1.0 + jnp.tanh(c * (h + 0.044715 * (h * h * h)))))


def _rope_factors(t):
    lane = lax.broadcasted_iota(jnp.int32, t.shape, 1) & (C_HEAD_DIM - 1)
    cosf = jnp.where((lane >= ROPE_DIM) & (lane < 2 * ROPE_DIM), 1.0, t)
    sinf = jnp.where(lane < ROPE_DIM, pltpu.roll(t, LANES - ROPE_DIM, axis=1), 0.0)
    return cosf, sinf, lane < ROPE_DIM // 2


def _rope_slab(hc, cosf, sinf, first_half):
    up = pltpu.roll(hc, LANES - ROPE_DIM // 2, axis=1)
    dn = pltpu.roll(hc, ROPE_DIM // 2, axis=1)
    return hc * cosf + jnp.where(first_half, up, dn) * sinf


ROW_CHUNK = 256


def _inproj_kernel(x_ref, w_ref, tab_ref, ha_ref, misc_ref, g0_ref, g1_ref, g2_ref, tabs_ref, acc_ref):
    j = pl.program_id(1)
    tm = x_ref.shape[0]

    @pl.when(j == 0)
    def _():
        tabs_ref[0] = tab_ref[...]
        for g, dil in enumerate(C_DILATIONS[1:], start=1):
            n = tm // dil
            for r in range(dil):
                tabs_ref[g, r * n:(r + 1) * n, :] = tab_ref[pl.ds(r, n, stride=dil), :]

    def chunks(size=ROW_CHUNK):
        for rc in range(tm // size):
            rows = slice(rc * size, (rc + 1) * size)
            yield rc, rows, jnp.dot(x_ref[rows, :], w_ref[...], preferred_element_type=F32)

    def fourier_in():
        for _, rows, res in chunks():
            ha_ref[rows, :] = res

    def gate_path():
        for _, rows, res in chunks():
            misc_ref[rows, :] = _silu(res).astype(BF16)

    def gating_in():
        for _, rows, res in chunks():
            misc_ref[rows, :] = _gelu_tanh(res).astype(BF16)

    def qkv(g):
        dil, out_ref = C_DILATIONS[g], (g0_ref, g1_ref, g2_ref)[g]
        part = j - (J_QKV + 3 * g)
        is_v = part == 2
        scale = jnp.where(part == 0, Q_SCALE, 1.0).astype(F32)
        size = max(ROW_CHUNK, BF16_ROWS * dil)
        n = size // dil
        for rc, rows, res in chunks(size):
            if dil > 1:
                for c, slab in enumerate(_slabs(res)):
                    acc_ref[rc % 2, c, :size, :] = slab
            for r in range(dil):
                cosf, sinf, first_half = _rope_factors(tabs_ref[g, pl.ds(r * (tm // dil) + rc * n, n), :])
                cosf = jnp.where(is_v, 1.0, cosf) * scale
                sinf = jnp.where(is_v, 0.0, sinf) * scale
                for c in range(NSLAB):
                    cols = slice(c * LANES, (c + 1) * LANES)
                    if dil > 1:
                        hc = acc_ref[rc % 2, c, pl.ds(r, n, stride=dil), :]
                    else:
                        hc = res[:, cols]
                    hc = _rope_slab(hc, cosf, sinf, first_half)
                    if dil > 1:
                        out_ref[r, rc * n:(rc + 1) * n, cols] = hc.astype(BF16)
                    else:
                        out_ref[rows, cols] = hc.astype(BF16)

    leaves = [(0, fourier_in), (J_MISC + MISC_AZ, gate_path), (J_MISC + MISC_U, gating_in)]
    leaves += [(J_QKV + 3 * g, functools.partial(qkv, g)) for g in range(C_NG)]

    def dispatch(sub):
        if len(sub) == 1:
            sub[0][1]()
            return
        mid = len(sub) // 2
        pl.when(j < sub[mid][0])(lambda: dispatch(sub[:mid]))
        pl.when(j >= sub[mid][0])(lambda: dispatch(sub[mid:]))

    dispatch(leaves)


MIX_BLOCK_OF = tuple(off // CB for off in (
    [OFF_A, OFF_AZ, OFF_BZ, OFF_CZ, OFF_B, OFF_B + CB]
    + [OFF_C + part * C_QKV + g * CB for g in range(C_NG) for part in range(3)]))


def _mix_block(j):
    blk = jnp.int32(MIX_BLOCK_OF[0])
    for step in range(1, N_MIX_BLOCKS):
        blk = jnp.where(j == step, MIX_BLOCK_OF[step], blk)
    return blk


def _inproj(x2, w_all, layer, rope_tab, batch, seq, tm):
    m = x2.shape[0]
    nseq = seq // tm

    def part(j, first):
        return jnp.clip(j - first, 0, 2)

    def dil_spec(g):
        dil = C_DILATIONS[g]
        return pl.BlockSpec((None, dil, tm // dil, CB),
                            lambda i, j: (i // nseq, 0, i % nseq, part(j, J_QKV + 3 * g)))

    def dil_shape(g):
        dil = C_DILATIONS[g]
        return jax.ShapeDtypeStruct((batch, dil, seq // dil, 3 * CB), BF16)

    return pl.pallas_call(
        _inproj_kernel,
        grid=(m // tm, N_MIX_BLOCKS),
        in_specs=[
            pl.BlockSpec((tm, D_MODEL), lambda i, j: (i, 0)),
            pl.BlockSpec((None, D_MODEL, CB), lambda i, j: (layer, 0, _mix_block(j))),
            pl.BlockSpec((tm, LANES), lambda i, j: (i % nseq, 0)),
        ],
        out_specs=[
            pl.BlockSpec((tm, CB), lambda i, j: (i, 0)),
            pl.BlockSpec((tm, CB), lambda i, j: (i, jnp.clip(j - J_MISC, 0, N_MISC - 1))),
            pl.BlockSpec((tm, CB), lambda i, j: (i, part(j, J_QKV))),
            dil_spec(1), dil_spec(2),
        ],
        out_shape=[
            jax.ShapeDtypeStruct((m, CB), F32),
            jax.ShapeDtypeStruct((m, N_MISC * CB), BF16),
            jax.ShapeDtypeStruct((m, 3 * CB), BF16),
            dil_shape(1), dil_shape(2),
        ],
        scratch_shapes=[pltpu.VMEM((C_NG, tm, LANES), F32), pltpu.VMEM((2, NSLAB, ROW_CHUNK, LANES), F32)],
        compiler_params=_params("parallel", "arbitrary"),
        name="inproj",
    )(x2, w_all, rope_tab)


def _sgu_kernel(u_ref, v_ref, z_ref, g_ref, b_ref, ws_ref, bs_ref, o_ref):
    tm = u_ref.shape[0]
    v = v_ref[...].astype(F32)
    mu = jnp.mean(v, axis=-1, keepdims=True)
    vc = v - mu
    var = jnp.mean(vc * vc, axis=-1, keepdims=True)
    vn = (vc * lax.rsqrt(var + LN_EPS) * g_ref[...] + b_ref[...]).astype(BF16)
    bias = bs_ref[...]
    nchunk = tm // B_CHUNK
    mixed = []
    for g in range(A_GROUPS):
        cols = slice(g * GDIM, (g + 1) * GDIM)
        side = jnp.concatenate([vn[n * B_CHUNK:(n + 1) * B_CHUNK, cols] for n in range(nchunk)], axis=1)
        mixed.append(jnp.dot(ws_ref[g], side, preferred_element_type=F32))
    for n in range(nchunk):
        rows = slice(n * B_CHUNK, (n + 1) * B_CHUNK)
        mix = jnp.concatenate([mixed[g][:, n * GDIM:(n + 1) * GDIM] for g in range(A_GROUPS)], axis=1) + bias
        o_ref[rows, :] = (u_ref[rows, :].astype(F32) * mix * z_ref[rows, :].astype(F32)).astype(BF16)


def _sgu(misc, ln_g, ln_b, ws_bf, bias_full, tm):
    m = misc.shape[0]
    col = lambda blk: pl.BlockSpec((tm, CB), lambda i: (i, blk))
    full = lambda shape: pl.BlockSpec(shape, lambda i: (0,) * len(shape))
    return pl.pallas_call(
        _sgu_kernel,
        grid=(m // tm,),
        in_specs=[col(MISC_U), col(MISC_V), col(MISC_BZ), full((1, CB)), full((1, CB)),
                  full((A_GROUPS, B_CHUNK, B_CHUNK)), full((B_CHUNK, CB))],
        out_specs=pl.BlockSpec((tm, CB), lambda i: (i, 0)),
        out_shape=jax.ShapeDtypeStruct((m, CB), BF16),
        compiler_params=_params("parallel"),
        name="sgu",
    )(misc, misc, misc, ln_g, ln_b, ws_bf, bias_full)


SA_ROWS = 8
SA_BLOCKS = 4
SB_ROWS = BF16_ROWS


def _fft_a_kernel(x0_ref, x1_ref, x2_ref, x3_ref, f_ref, y_ref, xin_ref, yout_ref):
    n1 = x0_ref.shape[0]
    for blk in range(SA_BLOCKS):
        srows = slice(blk * SA_ROWS, (blk + 1) * SA_ROWS)
        for g, x_ref in enumerate((x0_ref, x1_ref, x2_ref, x3_ref)):
            xin_ref[blk, g] = x_ref[:, srows, :].reshape(n1 * SA_ROWS, LANES)
        for jj in range(SA_ROWS):
            xj = jnp.concatenate(
                [xin_ref[blk, g, pl.ds(jj, n1, stride=SA_ROWS), :] for g in range(A_GROUPS)],
                axis=1).astype(BF16)
            yj = jnp.dot(f_ref[...], xj, preferred_element_type=F32)
            for g, slab in enumerate(_slabs(yj)):
                yout_ref[blk, g, pl.ds(jj, 2 * n1, stride=SA_ROWS), :] = slab
        for g in range(A_GROUPS):
            y_ref[g, :, srows, :] = yout_ref[blk, g].reshape(2 * n1, SA_ROWS, LANES)


def _fft_b_kernel(yr_ref, yi_ref, m_ref, z_ref, cs_ref, wm_ref, o_ref, f_ref, csw_ref, *, scale):
    n2 = yr_ref.shape[2]
    for g in range(A_GROUPS):
        csw_ref[g] = (jnp.dot(cs_ref[...], wm_ref[g], preferred_element_type=F32) * scale).astype(BF16)

    def body(kk, carry):
        ycat = jnp.concatenate(
            [jnp.concatenate([y_ref[g, kk] for g in range(A_GROUPS)], axis=1) for y_ref in (yr_ref, yi_ref)],
            axis=0).astype(BF16)
        zz = jnp.dot(m_ref[kk], ycat, preferred_element_type=F32).astype(BF16)
        for g in range(A_GROUPS):
            cols = slice(g * GDIM, (g + 1) * GDIM)
            z_ri = jnp.concatenate([zz[:n2, cols], zz[n2:, cols]], axis=1)
            f_ref[g, pl.ds(kk, n2, stride=SB_ROWS), :] = jnp.dot(
                z_ri, csw_ref[g], preferred_element_type=F32)
        return carry

    lax.fori_loop(0, SB_ROWS, body, 0, unroll=True)
    mixed = jnp.concatenate([f_ref[g] for g in range(A_GROUPS)], axis=1)
    gate = z_ref[...].reshape(n2 * SB_ROWS, CB).astype(F32)
    o_ref[...] = (mixed * gate).astype(BF16).reshape(n2, SB_ROWS, CB)


def _fourier(h_a, misc, f1_tab, m_tab, cs_tab, wmap_bf, batch, seq):
    n2 = B_CHUNK
    n1 = seq // n2
    x4 = h_a.reshape(batch, n1, n2, CB)
    step_rows = SA_ROWS * SA_BLOCKS
    xspec = lambda g: pl.BlockSpec((None, n1, step_rows, LANES), lambda b, t: (b, 0, t, g))
    y = pl.pallas_call(
        _fft_a_kernel,
        grid=(batch, n2 // step_rows),
        in_specs=[xspec(0), xspec(1), xspec(2), xspec(3),
                  pl.BlockSpec((2 * n1, n1), lambda b, t: (0, 0))],
        out_specs=pl.BlockSpec((None, A_GROUPS, 2 * n1, step_rows, LANES), lambda b, t: (b, 0, 0, t, 0)),
        out_shape=jax.ShapeDtypeStruct((batch, A_GROUPS, 2 * n1, n2, LANES), F32),
        scratch_shapes=[pltpu.VMEM((SA_BLOCKS, A_GROUPS, n1 * SA_ROWS, LANES), F32),
                        pltpu.VMEM((SA_BLOCKS, A_GROUPS, 2 * n1 * SA_ROWS, LANES), F32)],
        compiler_params=_params("parallel", "parallel"),
        name="fft_a",
    )(x4, x4, x4, x4, f1_tab)
    nk = n1 // SB_ROWS
    z4 = misc.reshape(batch, n2, n1, N_MISC * CB)
    out = pl.pallas_call(
        functools.partial(_fft_b_kernel, scale=1.0 / math.sqrt(seq * GDIM)),
        grid=(nk, batch),
        in_specs=[pl.BlockSpec((None, A_GROUPS, SB_ROWS, n2, LANES), lambda k, b: (b, 0, k, 0, 0)),
                  pl.BlockSpec((None, A_GROUPS, SB_ROWS, n2, LANES), lambda k, b: (b, 0, nk + k, 0, 0)),
                  pl.BlockSpec((SB_ROWS, 2 * n2, 2 * n2), lambda k, b: (k, 0, 0)),
                  pl.BlockSpec((None, n2, SB_ROWS, CB), lambda k, b: (b, 0, k, MISC_AZ)),
                  pl.BlockSpec((2 * GDIM, GDIM), lambda k, b: (0, 0)),
                  pl.BlockSpec((A_GROUPS, GDIM, GDIM), lambda k, b: (0, 0, 0))],
        out_specs=pl.BlockSpec((None, n2, SB_ROWS, CB), lambda k, b: (b, 0, k, 0)),
        out_shape=jax.ShapeDtypeStruct((batch, n2, n1, CB), BF16),
        scratch_shapes=[pltpu.VMEM((A_GROUPS, n2 * SB_ROWS, LANES), F32),
                        pltpu.VMEM((A_GROUPS, 2 * GDIM, GDIM), BF16)],
        compiler_params=_params("parallel", "parallel"),
        name="fft_b",
    )(y, y, m_tab, z4, cs_tab, wmap_bf)
    return out.reshape(batch * seq, CB)


def _dft_tables(seq):
    n2 = B_CHUNK
    n1 = seq // n2
    two_pi = 2.0 * math.pi

    def cs(idx, period):
        ang = (idx % period).astype(F32) * (two_pi / period)
        return jnp.cos(ang), jnp.sin(ang)

    i1 = jnp.arange(n1, dtype=jnp.int32)
    c1, s1 = cs(i1[:, None] * i1[None, :], n1)
    f1_tab = jnp.concatenate([c1, -s1], axis=0).astype(BF16)
    k = i1[:, None, None] + n1 * jnp.arange(n2, dtype=jnp.int32)[None, :, None]
    s2 = jnp.arange(n2, dtype=jnp.int32)[None, None, :]
    cm, sm = cs(k * s2, seq)
    m_tab = jnp.concatenate([jnp.concatenate([cm, sm], axis=2),
                             jnp.concatenate([-sm, cm], axis=2)], axis=1).astype(BF16)
    ic = jnp.arange(GDIM, dtype=jnp.int32)
    cc, sc = cs(ic[:, None] * ic[None, :], GDIM)
    cs_tab = jnp.concatenate([cc, sc], axis=0).astype(BF16)
    return f1_tab, m_tab, cs_tab


Q_SUB = 128
LSE_LANES = LANES // C_HEADS
ATTN_ROWS = 2048


def _attn_kernel(q_ref, kc_ref, kp_ref, kn_ref, vc_ref, vp_ref, vn_ref, o_ref, l_ref, *, seq_len, tq):
    t = pl.program_id(2)
    tk = Q_SUB + 2 * HALF_WIN
    ones = jnp.ones((tk, LANES), BF16)
    row = lax.broadcasted_iota(jnp.int32, (2 * Q_SUB, tk), 0) & (Q_SUB - 1)
    col = lax.broadcasted_iota(jnp.int32, (2 * Q_SUB, tk), 1)
    band = jnp.abs(row + HALF_WIN - col) <= HALF_WIN
    lane = lax.broadcasted_iota(jnp.int32, (Q_SUB, LANES), 1)
    lo = lane < C_HEAD_DIM
    head_of_lane = lane // LSE_LANES
    for rr in range(q_ref.shape[0]):
        k = jnp.concatenate([kp_ref[rr], kc_ref[rr], kn_ref[rr]], axis=0)
        v = jnp.concatenate([vp_ref[rr], vc_ref[rr], vn_ref[rr]], axis=0)
        for u in range(tq // Q_SUB):
            rows = slice(u * Q_SUB, (u + 1) * Q_SUB)
            kpos = t * tq + u * Q_SUB - HALF_WIN + col[:1]
            valid = band & ((kpos >= 0) & (kpos < seq_len))
            q = q_ref[rr, rows, :]
            ku, vu = k[u * Q_SUB:u * Q_SUB + tk], v[u * Q_SUB:u * Q_SUB + tk]
            lse_all = jnp.zeros((Q_SUB, LANES), F32)
            for hp in range(C_HEADS // 2):
                cols = slice(hp * LANES, (hp + 1) * LANES)
                qp = q[:, cols]
                zero = jnp.zeros_like(qp)
                q2 = jnp.concatenate([jnp.where(lo, qp, zero), jnp.where(lo, zero, qp)], axis=0)
                s = lax.dot_general(q2, ku[:, cols], (((1,), (1,)), ((), ())), preferred_element_type=F32)
                s = jnp.where(valid, s, NEG_BIG)
                m = jnp.max(s, axis=1, keepdims=True)
                p = jnp.exp2(s - m).astype(BF16)
                pv = jnp.dot(p, jnp.concatenate([vu[:, cols], ones], axis=1), preferred_element_type=F32)
                den = pv[:, LANES:]
                o = pv[:, :LANES] / den
                lse = m * math.log(2.0) + jnp.log(den)
                o_ref[rr, rows, cols] = jnp.where(lo, o[:Q_SUB], o[Q_SUB:]).astype(BF16)
                lse_all = jnp.where(head_of_lane == 2 * hp, lse[:Q_SUB], lse_all)
                lse_all = jnp.where(head_of_lane == 2 * hp + 1, lse[Q_SUB:], lse_all)
            l_ref[rr, rows, :] = lse_all


def _attn_group(qkv, g):
    batch, dil, ln, _ = qkv.shape
    tq = min(ATTN_ROWS, ln)
    nres = min(dil, ATTN_ROWS // tq)
    per = tq // HALF_WIN
    nhalo = ln // HALF_WIN

    def cur(blk):
        return pl.BlockSpec((None, nres, tq, CB), lambda b, r, t: (b, r, t, blk))

    def before(blk):
        return pl.BlockSpec((None, nres, HALF_WIN, CB),
                            lambda b, r, t: (b, r, jnp.maximum(t * per - 1, 0), blk))

    def after(blk):
        return pl.BlockSpec((None, nres, HALF_WIN, CB),
                            lambda b, r, t: (b, r, jnp.minimum((t + 1) * per, nhalo - 1), blk))

    tile = pl.BlockSpec((None, nres, tq, CB), lambda b, r, t: (b, r, t, 0))
    ltile = pl.BlockSpec((None, nres, tq, LANES), lambda b, r, t: (b, r, t, 0))
    return pl.pallas_call(
        functools.partial(_attn_kernel, seq_len=ln, tq=tq),
        grid=(batch, dil // nres, ln // tq),
        in_specs=[cur(0), cur(1), before(1), after(1), cur(2), before(2), after(2)],
        out_specs=[tile, ltile],
        out_shape=[jax.ShapeDtypeStruct((batch, dil, ln, CB), BF16),
                   jax.ShapeDtypeStruct((batch, dil, ln, LANES), F32)],
        compiler_params=_params("parallel", "parallel", "parallel"),
        name=f"attn{g}",
    )(*([qkv] * 7))


def _combine_kernel(o0_ref, l0_ref, o1_ref, l1_ref, o2_ref, l2_ref, cz_ref, y_ref, os1_ref, os2_ref, ls_ref):
    tm = o0_ref.shape[0]
    for gi, (o_ref, l_ref, os_ref) in enumerate(((o1_ref, l1_ref, os1_ref), (o2_ref, l2_ref, os2_ref))):
        dil = C_DILATIONS[gi + 1]
        n = tm // dil
        for r in range(dil):
            rows = pl.ds(r, n, stride=dil)
            ls_ref[gi, rows, :] = l_ref[r]
            for c in range(NSLAB):
                os_ref[c, rows, :] = o_ref[r, :, c * LANES:(c + 1) * LANES].astype(F32)
    l0, l1, l2 = l0_ref[...], ls_ref[0], ls_ref[1]
    mx = jnp.maximum(jnp.maximum(l0, l1), l2)
    es = [jnp.exp(l - mx) for l in (l0, l1, l2)]
    den = es[0] + es[1] + es[2]
    spread = (lax.broadcasted_iota(jnp.int32, (LANES, CB), 0)
              == (lax.broadcasted_iota(jnp.int32, (LANES, CB), 1) // C_HEAD_DIM) * LSE_LANES).astype(BF16)
    wide = []
    for e in es:
        w = e / den
        hi = w.astype(BF16)
        lo = (w - hi.astype(F32)).astype(BF16)
        wide.append(jnp.dot(hi, spread, preferred_element_type=F32) + jnp.dot(lo, spread, preferred_element_type=F32))
    for c in range(NSLAB):
        cols = slice(c * LANES, (c + 1) * LANES)
        mix = (wide[0][:, cols] * o0_ref[:, cols].astype(F32) + wide[1][:, cols] * os1_ref[c]
               + wide[2][:, cols] * os2_ref[c])
        y_ref[:, cols] = (mix * cz_ref[:, cols].astype(F32)).astype(BF16)


def _combine(res, misc, batch, seq, tm):
    (o0, l0), (o1, l1), (o2, l2) = res
    nseq = seq // tm
    def spec(dil, width):
        if dil == 1:
            return pl.BlockSpec((None, None, tm, width), lambda i: (i // nseq, 0, i % nseq, 0))
        return pl.BlockSpec((None, dil, tm // dil, width), lambda i: (i // nseq, 0, i % nseq, 0))

    in_specs = [spec(dil, width) for dil in C_DILATIONS for width in (CB, LANES)]
    return pl.pallas_call(
        _combine_kernel,
        grid=(batch * nseq,),
        in_specs=in_specs + [pl.BlockSpec((tm, CB), lambda i: (i, MISC_CZ))],
        out_specs=pl.BlockSpec((tm, CB), lambda i: (i, 0)),
        out_shape=jax.ShapeDtypeStruct((batch * seq, CB), BF16),
        scratch_shapes=[pltpu.VMEM((NSLAB, tm, LANES), F32), pltpu.VMEM((NSLAB, tm, LANES), F32),
                        pltpu.VMEM((C_NG - 1, tm, LANES), F32)],
        compiler_params=_params("parallel"),
        name="combine",
    )(o0, l0, o1, l1, o2, l2, misc)


def _merge_kernel(x_ref, ya_ref, yb_ref, yc_ref, wg0_ref, wg1_ref, wg2_ref, wb_ref, o_ref):
    xb = x_ref[...]
    acc = None
    for kk, (y_ref, wg_ref) in enumerate(((ya_ref, wg0_ref), (yb_ref, wg1_ref), (yc_ref, wg2_ref))):
        gate = jax.nn.sigmoid(jnp.dot(xb, wg_ref[...], preferred_element_type=F32))
        term = gate * jnp.dot(y_ref[...], wb_ref[kk], preferred_element_type=F32)
        acc = term if acc is None else acc + term
    o_ref[...] = acc.astype(BF16)


def _merge(x2, ya, yb, yc, w_all, wbr_all, layer, tm):
    m = x2.shape[0]
    ncol = D_MODEL // CB
    ybs = pl.BlockSpec((tm, CB), lambda i, c: (i, 0))

    def gate_spec(kk):
        return pl.BlockSpec((None, D_MODEL, CB), lambda i, c: (layer, 0, OFF_G // CB + kk * ncol + c))

    return pl.pallas_call(
        _merge_kernel,
        grid=(m // tm, ncol),
        in_specs=[pl.BlockSpec((tm, D_MODEL), lambda i, c: (i, 0)), ybs, ybs, ybs,
                  gate_spec(0), gate_spec(1), gate_spec(2),
                  pl.BlockSpec((None, N_BRANCH, W_BR, CB), lambda i, c: (layer, 0, 0, c))],
        out_specs=pl.BlockSpec((tm, CB), lambda i, c: (i, c)),
        out_shape=jax.ShapeDtypeStruct((m, D_MODEL), BF16),
        compiler_params=_params("parallel", "parallel"),
        name="merge",
    )(x2, ya, yb, yc, w_all, w_all, w_all, wbr_all)


def _outproj_kernel(x_ref, m_ref, wo_ref, g_ref, b_ref, o_ref, *ob_ref):
    for rc in range(x_ref.shape[0] // ROW_CHUNK):
        rows = slice(rc * ROW_CHUNK, (rc + 1) * ROW_CHUNK)
        out = jnp.dot(m_ref[rows, :], wo_ref[...], preferred_element_type=F32)
        z = ALPHA * x_ref[rows, :] + out
        mu = jnp.mean(z, axis=-1, keepdims=True)
        zc = z - mu
        var = jnp.mean(zc * zc, axis=-1, keepdims=True)
        y = zc * lax.rsqrt(var + LN_EPS) * g_ref[...] + b_ref[...]
        o_ref[rows, :] = y
        if ob_ref:
            ob_ref[0][rows, :] = y.astype(BF16)


def _outproj(x2, merged, wo_all, layer, ln_g, ln_b, tm, with_bf16):
    m = x2.shape[0]
    row = pl.BlockSpec((tm, D_MODEL), lambda i: (i, 0))
    vec = pl.BlockSpec((1, D_MODEL), lambda i: (0, 0))
    return pl.pallas_call(
        _outproj_kernel,
        grid=(m // tm,),
        in_specs=[row, row,
                  pl.BlockSpec((None, D_MODEL, D_MODEL), lambda i: (layer, 0, 0), pipeline_mode=pl.Buffered(1)),
                  vec, vec],
        out_specs=[row, row] if with_bf16 else row,
        out_shape=([jax.ShapeDtypeStruct((m, D_MODEL), F32), jax.ShapeDtypeStruct((m, D_MODEL), BF16)]
                   if with_bf16 else jax.ShapeDtypeStruct((m, D_MODEL), F32)),
        compiler_params=_params("parallel"),
        name="outproj",
    )(x2, merged, wo_all, ln_g, ln_b)


def _rope_table(seq):
    half = ROPE_DIM // 2
    inv_freq = ROPE_THETA ** (-jnp.arange(half, dtype=F32) / half)
    ang = jnp.arange(seq, dtype=F32)[:, None] * inv_freq[None, :]
    cos, sin = jnp.cos(ang), jnp.sin(ang)
    ones = jnp.ones((seq, C_HEAD_DIM - 2 * ROPE_DIM), F32)
    head = jnp.concatenate([cos, cos, -sin, sin, ones], axis=1)
    return jnp.tile(head, (1, LANES // C_HEAD_DIM))


TM_INPROJ = 2048
TM_SGU = 2048
TM_COMBINE = 1024
TM_MERGE = 1024
TM_OUTPROJ = 512


def _layer(x2, x_mm, batch, seq, wl, rope_tab, dft_t):
    h_a, misc, g0, g1, g2 = _inproj(x_mm, wl["w_in"], wl["layer"], rope_tab, batch, seq, TM_INPROJ)
    y_a = _fourier(h_a, misc, *dft_t, wl["w_amap"], batch, seq)
    y_b = _sgu(misc, wl["sgu_ln_g"], wl["sgu_ln_b"], wl["sgu_w"], wl["sgu_bias"], TM_SGU)
    groups = (g0.reshape(batch, 1, seq, 3 * CB), g1, g2)
    y_c = _combine([_attn_group(qkv, g) for g, qkv in enumerate(groups)], misc, batch, seq, TM_COMBINE)
    merged = _merge(x_mm, y_a, y_b, y_c, wl["w_in"], wl["w_branch"], wl["layer"], TM_MERGE)
    last = wl["layer"] == DEPTH - 1
    res = _outproj(x2, merged, wl["w_out"], wl["layer"], wl["ln_g"], wl["ln_b"], TM_OUTPROJ, not last)
    return (res, res) if last else tuple(res)


def kernel(x_prompt, x_sample, w_in, w_amap, sgu_ln_g, sgu_ln_b, sgu_w, sgu_b, w_branch, w_out, ln_g, ln_b):
    w_in_bf, w_branch_bf, w_out_bf = w_in.astype(BF16), w_branch.astype(BF16), w_out.astype(BF16)
    layers = []
    for l in range(DEPTH):
        layers.append({
            "layer": l,
            "w_in": w_in_bf,
            "w_amap": w_amap[l].astype(BF16),
            "sgu_ln_g": sgu_ln_g[l][None, :],
            "sgu_ln_b": sgu_ln_b[l][None, :],
            "sgu_w": sgu_w[l].astype(BF16),
            "sgu_bias": jnp.repeat(sgu_b[l].T, GDIM, axis=1),
            "w_branch": w_branch_bf,
            "w_out": w_out_bf,
            "ln_g": ln_g[l][None, :],
            "ln_b": ln_b[l][None, :],
        })
    outs = []
    for x in (x_prompt, x_sample):
        batch, seq, _ = x.shape
        rope_t = _rope_table(seq)
        dft_t = _dft_tables(seq)
        x2 = x.reshape(batch * seq, D_MODEL)
        x_mm = x2.astype(BF16)
        for wl in layers:
            x2, x_mm = _layer(x2, x_mm, batch, seq, wl, rope_t, dft_t)
        outs.append(x2.reshape(batch, seq, D_MODEL))
    return tuple(outs)
```
